```python
import jax, jax.numpy as jnp
from jax import lax
import numpy as np

D_MODEL = 1024
BATCH = 4
SEQ = 8192
DEPTH = 2

MEM_LEN = 256
N_MIXERS = 2
SB_HEADS = 12
SB_HEAD_DIM = 64
SB_WIDTH = SB_HEADS * SB_HEAD_DIM
SB_BLOCK = 128
DN_HEADS = 6
DN_HEAD_DIM = 128
DN_WIDTH = DN_HEADS * DN_HEAD_DIM
DN_CONV = 4
DN_CHUNK = 64
XA_HEADS = 4
XA_HEAD_DIM = 64
XA_WIDTH = XA_HEADS * XA_HEAD_DIM
MIX_WIDTH = SB_WIDTH + XA_WIDTH
SB_IN = 3 * SB_WIDTH + XA_WIDTH
DN_IN = 4 * DN_WIDTH + 2 * DN_HEADS + XA_WIDTH
D_FF = 3584
N_EXPERTS = 8
TOP_K = 2
MOE_BLOCK = 128
EPS = 1e-6

kernel_name = "hybrid_sb_gdn_xattn_moe"

F32 = jnp.float32


def rmsnorm(x, g):
    xf = x.astype(F32)
    y = xf * lax.rsqrt(jnp.mean(xf * xf, axis=-1, keepdims=True) + EPS) * g.astype(F32)
    return y.astype(x.dtype)


def l2norm(x):
    return x * lax.rsqrt(jnp.sum(x * x, axis=-1, keepdims=True) + EPS)


def causal_depthwise_conv(x, w):
    c = x.shape[-1]
    return lax.conv_general_dilated(
        x, w[:, None, :].astype(x.dtype), window_strides=(1,),
        padding=[(w.shape[0] - 1, 0)], dimension_numbers=('NWC', 'WIO', 'NWC'),
        feature_group_count=c)


def stick_breaking_attention(q, k, v):
    b, s, h, d = q.shape
    nb = s // SB_BLOCK
    scale = d ** -0.5
    kf = k.astype(F32).transpose(0, 2, 1, 3)
    vt = v.transpose(0, 2, 1, 3)
    qb = q.reshape(b, nb, SB_BLOCK, h, d).transpose(1, 0, 3, 2, 4)
    key_pos = jnp.arange(s)

    def block(args):
        q_blk, blk = args
        q_pos = blk * SB_BLOCK + jnp.arange(SB_BLOCK)
        z = jnp.einsum('bhqd,bhkd->bhqk', q_blk.astype(F32), kf) * scale
        causal = key_pos[None, :] < q_pos[:, None]
        log_1m = jnp.where(causal, jax.nn.log_sigmoid(-z), 0.0)
        after = lax.cumsum(log_1m, axis=3, reverse=True) - log_1m
        a = jnp.where(causal, jnp.exp(jax.nn.log_sigmoid(z) + after), 0.0)
        return jnp.einsum('bhqk,bhkd->bhqd', a.astype(vt.dtype), vt)

    o = lax.map(block, (qb, jnp.arange(nb)))
    return o.transpose(1, 0, 3, 2, 4).reshape(b, s, h * d)


def gated_delta_rule(q, k, v, g, beta):
    b, s, h, dk = q.shape
    dv = v.shape[-1]
    c = DN_CHUNK
    n = s // c

    def to_chunks(t):
        return t.reshape(b, n, c, h, t.shape[-1]).transpose(0, 3, 1, 2, 4)

    def to_chunks_s(t):
        return t.reshape(b, n, c, h).transpose(0, 3, 1, 2)

    qc, kc, vc = to_chunks(q), to_chunks(k), to_chunks(v)
    gc = lax.cumsum(to_chunks_s(g), axis=3)
    bc = to_chunks_s(beta)
    kb = kc * bc[..., None]
    vb = vc * bc[..., None]
    tri = jnp.tril(jnp.ones((c, c), bool))
    strict = jnp.tril(jnp.ones((c, c), bool), -1)
    decay = jnp.exp(jnp.where(tri, gc[..., :, None] - gc[..., None, :], -jnp.inf))
    kk = jnp.einsum('bhncd,bhnsd->bhncs', kb, kc) * decay
    lower = jnp.where(strict, kk, 0.0) + jnp.eye(c, dtype=F32)
    rhs = jnp.concatenate([vb, kb * jnp.exp(gc)[..., None]], axis=-1)
    sol = lax.linalg.triangular_solve(lower, rhs, left_side=True, lower=True, unit_diagonal=True)
    u, w = sol[..., :dv], sol[..., dv:]
    qk = jnp.where(tri, jnp.einsum('bhncd,bhnsd->bhncs', qc, kc) * decay, 0.0)
    q_dec = qc * jnp.exp(gc)[..., None]
    g_last = gc[..., -1]
    k_dec = kc * jnp.exp(g_last[..., None] - gc)[..., None]

    def step(state, xs):
        q_d, qk_c, u_c, w_c, k_d, gl = xs
        v_new = u_c - jnp.einsum('bhck,bhkv->bhcv', w_c, state)
        o = jnp.einsum('bhck,bhkv->bhcv', q_d, state) + jnp.einsum('bhcs,bhsv->bhcv', qk_c, v_new)
        state = state * jnp.exp(gl)[..., None, None] + jnp.einsum('bhck,bhcv->bhkv', k_d, v_new)
        return state, o

    xs = tuple(jnp.moveaxis(t, 2, 0) for t in (q_dec, qk, u, w, k_dec, g_last))
    s0 = jnp.zeros((b, h, dk, dv), F32)
    _, o = lax.scan(step, s0, xs)
    return o.transpose(1, 0, 3, 2, 4).reshape(b, s, h, dv)


def stick_breaking_mixer(hn, w_in, q_gain, k_gain):
    b, s, _ = hn.shape
    proj = hn @ w_in
    q = proj[..., :SB_WIDTH].reshape(b, s, SB_HEADS, SB_HEAD_DIM)
    k = proj[..., SB_WIDTH:2 * SB_WIDTH].reshape(b, s, SB_HEADS, SB_HEAD_DIM)
    v = proj[..., 2 * SB_WIDTH:3 * SB_WIDTH].reshape(b, s, SB_HEADS, SB_HEAD_DIM)
    xq = proj[..., 3 * SB_WIDTH:]
    o = stick_breaking_attention(rmsnorm(q, q_gain), rmsnorm(k, k_gain), v)
    return o, xq


def deltanet_mixer(hn, w_in, conv_w, a_log, dt_bias, o_gain):
    b, s, _ = hn.shape
    proj = hn @ w_in
    qkv = jax.nn.silu(causal_depthwise_conv(proj[..., :3 * DN_WIDTH], conv_w)).astype(F32)
    gate = proj[..., 3 * DN_WIDTH:4 * DN_WIDTH].reshape(b, s, DN_HEADS, DN_HEAD_DIM).astype(F32)
    a = proj[..., 4 * DN_WIDTH:4 * DN_WIDTH + DN_HEADS].astype(F32)
    bb = proj[..., 4 * DN_WIDTH + DN_HEADS:4 * DN_WIDTH + 2 * DN_HEADS].astype(F32)
    xq = proj[..., 4 * DN_WIDTH + 2 * DN_HEADS:]
    q = qkv[..., :DN_WIDTH].reshape(b, s, DN_HEADS, DN_HEAD_DIM)
    k = qkv[..., DN_WIDTH:2 * DN_WIDTH].reshape(b, s, DN_HEADS, DN_HEAD_DIM)
    v = qkv[..., 2 * DN_WIDTH:].reshape(b, s, DN_HEADS, DN_HEAD_DIM)
    q = l2norm(q) * (DN_HEAD_DIM ** -0.5)
    k = l2norm(k)
    beta = jax.nn.sigmoid(bb)
    g = -jnp.exp(a_log.astype(F32)) * jax.nn.softplus(a + dt_bias.astype(F32))
    o = gated_delta_rule(q, k, v, g, beta)
    o = rmsnorm(o, o_gain) * jax.nn.silu(gate)
    return o.reshape(b, s, DN_WIDTH).astype(hn.dtype), xq


def memory_cross_attention(xq, mem_kv, q_gain, k_gain):
    b, s, _ = xq.shape
    m = mem_kv.shape[1]
    q = rmsnorm(xq.reshape(b, s, XA_HEADS, XA_HEAD_DIM), q_gain)
    k = rmsnorm(mem_kv[..., :XA_WIDTH].reshape(b, m, XA_HEADS, XA_HEAD_DIM), k_gain)
    v = mem_kv[..., XA_WIDTH:].reshape(b, m, XA_HEADS, XA_HEAD_DIM)
    sc = jnp.einsum('bshd,bmhd->bhsm', q.astype(F32), k.astype(F32)) * (XA_HEAD_DIM ** -0.5)
    p = jax.nn.softmax(sc, axis=-1)
    o = jnp.einsum('bhsm,bmhd->bshd', p.astype(v.dtype), v)
    return o.reshape(b, s, XA_WIDTH)


def swiglu(x, w_gate, w_up, w_down):
    return (jax.nn.silu(x @ w_gate) * (x @ w_up)) @ w_down


def moe_swiglu(h, router, w_gate, w_up, w_down):
    b, s, d = h.shape
    t = b * s
    x = h.reshape(t, d)
    logits = x.astype(F32) @ router.astype(F32)
    top_logit, top_idx = lax.top_k(logits, TOP_K)
    top_w = jax.nn.softmax(top_logit, axis=-1)
    n_assign = t * TOP_K
    flat_e = top_idx.reshape(-1)
    flat_tok = jnp.repeat(jnp.arange(t, dtype=jnp.int32), TOP_K)
    flat_w = top_w.reshape(-1)
    order = jnp.argsort(flat_e)
    se, stok, sw = flat_e[order], flat_tok[order], flat_w[order]
    counts = jnp.bincount(flat_e, length=N_EXPERTS)
    padded = (counts + MOE_BLOCK - 1) // MOE_BLOCK * MOE_BLOCK
    start = jnp.cumsum(counts) - counts
    pend = jnp.cumsum(padded)
    pstart = pend - padded
    dest = pstart[se] + jnp.arange(n_assign) - start[se]
    nblk = -(-n_assign // MOE_BLOCK) + N_EXPERTS
    cap = nblk * MOE_BLOCK
    tok_buf = jnp.zeros((cap,), jnp.int32).at[dest].set(stok)
    w_buf = jnp.zeros((cap,), F32).at[dest].set(sw)
    blk_e = jnp.minimum(jnp.searchsorted(pend, jnp.arange(nblk) * MOE_BLOCK, side='right'), N_EXPERTS - 1)

    def block(args):
        tok, wt, e = args
        xb = x[tok]
        y = swiglu(xb, w_gate[e], w_up[e], w_down[e])
        return y * wt[:, None].astype(y.dtype)

    y = lax.map(block, (tok_buf.reshape(nblk, MOE_BLOCK), w_buf.reshape(nblk, MOE_BLOCK), blk_e))
    out = jax.ops.segment_sum(y.reshape(cap, d), tok_buf, num_segments=t)
    return out.reshape(b, s, d)


def setup_inputs(seed: int = 0) -> dict:
    key = jax.random.key(seed)
    ks = iter(jax.random.split(key, 32))
    n_sb = (DEPTH + 1) // 2
    n_dn = DEPTH // 2
    n_dense = (DEPTH + 1) // 2
    n_moe = DEPTH // 2

    def dense(shape, fan_in):
        return jax.random.normal(next(ks), shape, F32) * (fan_in ** -0.5)

    def gain(shape):
        return 1.0 + 0.02 * jax.random.normal(next(ks), shape, F32)

    x = jax.random.normal(next(ks), (BATCH, SEQ, D_MODEL), F32)
    mem = jax.random.normal(next(ks), (BATCH, MEM_LEN, D_MODEL), F32)
    mix_norm = gain((DEPTH, D_MODEL))
    ffn_norm = gain((DEPTH, D_MODEL))
    mem_norm = gain((DEPTH, D_MODEL))
    w_mem_kv = dense((DEPTH, D_MODEL, 2 * XA_WIDTH), D_MODEL)
    xa_q_norm = gain((DEPTH, XA_HEAD_DIM))
    xa_k_norm = gain((DEPTH, XA_HEAD_DIM))
    w_out = dense((DEPTH, MIX_WIDTH, D_MODEL), MIX_WIDTH)
    sb_w_in = dense((n_sb, D_MODEL, SB_IN), D_MODEL)
    sb_q_norm = gain((n_sb, SB_HEAD_DIM))
    sb_k_norm = gain((n_sb, SB_HEAD_DIM))
    dn_w_in = dense((n_dn, D_MODEL, DN_IN), D_MODEL)
    dn_conv = dense((n_dn, DN_CONV, 3 * DN_WIDTH), DN_CONV)
    dn_a_log = jnp.log(jax.random.uniform(next(ks), (n_dn, DN_HEADS), F32, 1.0, 16.0))
    dt = jnp.exp(jax.random.uniform(next(ks), (n_dn, DN_HEADS), F32, float(np.log(1e-3)), float(np.log(1e-1))))
    dn_dt_bias = jnp.log(jnp.expm1(dt))
    dn_o_norm = gain((n_dn, DN_HEAD_DIM))
    ffd_w_gate = dense((n_dense, D_MODEL, D_FF), D_MODEL)
    ffd_w_up = dense((n_dense, D_MODEL, D_FF), D_MODEL)
    ffd_w_down = dense((n_dense, D_FF, D_MODEL), D_FF)
    moe_router = dense((n_moe, D_MODEL, N_EXPERTS), D_MODEL)
    moe_w_gate = dense((n_moe, N_EXPERTS, D_MODEL, D_FF), D_MODEL)
    moe_w_up = dense((n_moe, N_EXPERTS, D_MODEL, D_FF), D_MODEL)
    moe_w_down = dense((n_moe, N_EXPERTS, D_FF, D_MODEL), D_FF)
    return {"x": x, "mem": mem, "mix_norm": mix_norm, "ffn_norm": ffn_norm, "mem_norm": mem_norm,
            "w_mem_kv": w_mem_kv, "xa_q_norm": xa_q_norm, "xa_k_norm": xa_k_norm, "w_out": w_out,
            "sb_w_in": sb_w_in, "sb_q_norm": sb_q_norm, "sb_k_norm": sb_k_norm,
            "dn_w_in": dn_w_in, "dn_conv": dn_conv, "dn_a_log": dn_a_log, "dn_dt_bias": dn_dt_bias,
            "dn_o_norm": dn_o_norm, "ffd_w_gate": ffd_w_gate, "ffd_w_up": ffd_w_up, "ffd_w_down": ffd_w_down,
            "moe_router": moe_router, "moe_w_gate": moe_w_gate, "moe_w_up": moe_w_up, "moe_w_down": moe_w_down}


def reference(x, mem, mix_norm, ffn_norm, mem_norm, w_mem_kv, xa_q_norm, xa_k_norm, w_out,
              sb_w_in, sb_q_norm, sb_k_norm, dn_w_in, dn_conv, dn_a_log, dn_dt_bias, dn_o_norm,
              ffd_w_gate, ffd_w_up, ffd_w_down, moe_router, moe_w_gate, moe_w_up, moe_w_down):
    h = x
    for i in range(DEPTH):
        j = i // N_MIXERS
        hn = rmsnorm(h, mix_norm[i])
        if i % N_MIXERS == 0:
            mix, xq = stick_breaking_mixer(hn, sb_w_in[j], sb_q_norm[j], sb_k_norm[j])
        else:
            mix, xq = deltanet_mixer(hn, dn_w_in[j], dn_conv[j], dn_a_log[j], dn_dt_bias[j], dn_o_norm[j])
        mem_kv = rmsnorm(mem, mem_norm[i]) @ w_mem_kv[i]
        xa = memory_cross_attention(xq, mem_kv, xa_q_norm[i], xa_k_norm[i])
        h = h + jnp.concatenate([mix, xa], axis=-1) @ w_out[i]
        hn = rmsnorm(h, ffn_norm[i])
        c = i // 2
        if i % 2 == 0:
            h = h + swiglu(hn, ffd_w_gate[c], ffd_w_up[c], ffd_w_down[c])
        else:
            h = h + moe_swiglu(hn, moe_router[c], moe_w_gate[c], moe_w_up[c], moe_w_down[c])
    return h
```

```python
import functools

import jax
import jax.numpy as jnp
from jax import lax
from jax.experimental import pallas as pl
from jax.experimental.pallas import tpu as pltpu

F32 = jnp.float32
BF16 = jnp.bfloat16
I32 = jnp.int32

EPS = 1e-6
D_MODEL = 1024
HEAD64 = 64
SB_HEADS = 12
SB_WIDTH = SB_HEADS * HEAD64
XA_WIDTH = 256
DN_HEADS = 6
DN_HEAD = 128
DN_WIDTH = DN_HEADS * DN_HEAD
DN_CONV = 4
DN_CHUNK = 64
D_FF = 3584
N_EXPERTS = 8

LANES = 128
MXU_DIM = 256
VMEM_LIMIT = 56 * 1024 * 1024

TM_PROJ = 512
TM_FFN = 1024
TF_FFN = 512
SB_TQ = 256
DN_ROWS = 256
MOE_BLK = 512


def _cparams(*sem):
    return pltpu.CompilerParams(dimension_semantics=sem, vmem_limit_bytes=VMEM_LIMIT)


def _dot(a, b):
    return jnp.dot(a, b, preferred_element_type=F32)


def _dot_nt(a, b):
    return lax.dot_general(a, b, (((1,), (1,)), ((), ())), preferred_element_type=F32)


def _dot_tn(a, b):
    return lax.dot_general(a, b, (((0,), (0,)), ((), ())), preferred_element_type=F32)


def _rms_rows(x, gain_row):
    ms = jnp.mean(x * x, axis=-1, keepdims=True)
    return x * lax.rsqrt(ms + EPS) * gain_row


def _softplus(x):
    return jnp.maximum(x, 0.0) + jnp.log1p(jnp.exp(-jnp.abs(x)))


def _sigmoid(x):
    return 1.0 / (1.0 + jnp.exp(-x))


def _head_sum_matrix(width, head):
    r = jnp.arange(width)[:, None] // head
    c = jnp.arange(width)[None, :] // head
    return (r == c).astype(BF16)


def _norm_proj_kernel(x_ref, g_ref, w_ref, hg_ref, hm_ref, *o_refs, n_norm_cols, n_main_cols):
    xn = _rms_rows(x_ref[...], g_ref[...]).astype(BF16)
    proj = _dot(xn, w_ref[...])
    o_ref = o_refs[0]
    for c in range(n_norm_cols // MXU_DIM):
        cols = slice(c * MXU_DIM, (c + 1) * MXU_DIM)
        blk = proj[:, cols]
        ss = _dot((blk * blk).astype(BF16), hm_ref[...])
        o_ref[:, cols] = (blk * lax.rsqrt(ss * (1.0 / HEAD64) + EPS) * hg_ref[:, cols]).astype(o_ref.dtype)
    if n_main_cols > n_norm_cols:
        o_ref[:, n_norm_cols:n_main_cols] = proj[:, n_norm_cols:n_main_cols].astype(o_ref.dtype)
    if len(o_refs) > 1:
        o_refs[1][...] = proj[:, n_main_cols:]


def _norm_proj(x, gain, w, head_gain, n_norm_cols, n_main_cols):
    t, d = x.shape
    n = w.shape[1]
    tm = min(TM_PROJ, t)
    out_shape = [jax.ShapeDtypeStruct((t, n_main_cols), BF16)]
    out_specs = [pl.BlockSpec((tm, n_main_cols), lambda i: (i, 0))]
    if n > n_main_cols:
        out_shape.append(jax.ShapeDtypeStruct((t, n - n_main_cols), F32))
        out_specs.append(pl.BlockSpec((tm, n - n_main_cols), lambda i: (i, 0)))
    hg = jnp.zeros((1, max(n_norm_cols, MXU_DIM)), F32) if head_gain is None else head_gain
    res = pl.pallas_call(
        functools.partial(_norm_proj_kernel, n_norm_cols=n_norm_cols, n_main_cols=n_main_cols),
        grid=(t // tm,),
        in_specs=[
            pl.BlockSpec((tm, d), lambda i: (i, 0)),
            pl.BlockSpec((1, d), lambda i: (0, 0)),
            pl.BlockSpec((d, n), lambda i: (0, 0)),
            pl.BlockSpec(hg.shape, lambda i: (0, 0)),
            pl.BlockSpec((MXU_DIM, MXU_DIM), lambda i: (0, 0)),
        ],
        out_specs=out_specs,
        out_shape=out_shape,
        compiler_params=_cparams("parallel"),
        name="norm_proj",
    )(x, gain.reshape(1, d), w, hg, _head_sum_matrix(MXU_DIM, HEAD64))
    return res


def _sb_block(qh, kblk, vblk, u, carry, causal):
    z = _dot_nt(qh, kblk)
    log_1m = -_softplus(z)
    log_b = z + log_1m
    if causal is not None:
        log_1m = jnp.where(causal, log_1m, 0.0)
    hi = log_1m.astype(BF16)
    lo = (log_1m - hi.astype(F32)).astype(BF16)
    within = _dot(hi, u) + _dot(lo, u)
    a = jnp.exp(log_b + within + carry)
    if causal is not None:
        a = jnp.where(causal, a, 0.0)
    pv = _dot(a.astype(BF16), vblk)
    return pv, carry + jnp.sum(log_1m, axis=1, keepdims=True)


def _sb_attn_kernel(q_ref, k_ref, v_ref, u_ref, o_ref):
    i = pl.program_id(2)
    tq = q_ref.shape[0]
    q = q_ref[...]
    lane = lax.broadcasted_iota(I32, (tq, LANES), 1)
    zero = jnp.zeros_like(q)
    qs = (jnp.where(lane < HEAD64, q, zero), jnp.where(lane >= HEAD64, q, zero))
    u = u_ref[...]
    row = lax.broadcasted_iota(I32, (tq, tq), 0)
    col = lax.broadcasted_iota(I32, (tq, tq), 1)
    causal = col < row

    start = pl.multiple_of(i * tq, tq)
    kd = k_ref[pl.ds(start, tq), :]
    vd = v_ref[pl.ds(start, tq), :]
    state = []
    for qh in qs:
        pv, carry = _sb_block(qh, kd, vd, u, jnp.zeros((tq, 1), F32), causal)
        state += [pv, carry]

    def body(jj, st):
        off = pl.multiple_of((i - 1 - jj) * tq, tq)
        kb = k_ref[pl.ds(off, tq), :]
        vb = v_ref[pl.ds(off, tq), :]
        new = []
        for h, qh in enumerate(qs):
            pv, carry = _sb_block(qh, kb, vb, u, st[2 * h + 1], None)
            new += [st[2 * h] + pv, carry]
        return tuple(new)

    st = lax.fori_loop(0, i, body, tuple(state))
    o_ref[...] = jnp.where(lane < HEAD64, st[0], st[2]).astype(o_ref.dtype)


def _sb_attention(proj, batch, seq):
    tq = min(SB_TQ, seq)
    nq = seq // tq
    n_pairs = SB_WIDTH // LANES
    j_up = jnp.arange(tq)[:, None]
    s_up = jnp.arange(tq)[None, :]
    u = (j_up > s_up).astype(BF16)
    return pl.pallas_call(
        _sb_attn_kernel,
        grid=(batch, n_pairs, nq),
        in_specs=[
            pl.BlockSpec((tq, LANES), lambda b, p, i: (b * nq + i, p)),
            pl.BlockSpec((seq, LANES), lambda b, p, i: (b, n_pairs + p)),
            pl.BlockSpec((seq, LANES), lambda b, p, i: (b, 2 * n_pairs + p)),
            pl.BlockSpec((tq, tq), lambda b, p, i: (0, 0)),
        ],
        out_specs=pl.BlockSpec((tq, LANES), lambda b, p, i: (b * nq + i, p)),
        out_shape=jax.ShapeDtypeStruct((batch * seq, SB_WIDTH), BF16),
        compiler_params=_cparams("parallel", "parallel", "arbitrary"),
        name="sb_attention",
    )(proj, proj, proj, u)


def _xattn_out_kernel(h_ref, mix_ref, xq_ref, mk_ref, mv_ref, qg_ref, hm_ref, wo_ref, o_ref):
    tm = h_ref.shape[0]
    xq = xq_ref[...].astype(F32)
    ss = _dot((xq * xq).astype(BF16), hm_ref[...])
    qn = (xq * lax.rsqrt(ss * (1.0 / HEAD64) + EPS) * qg_ref[...]).astype(BF16)
    lane = lax.broadcasted_iota(I32, (tm, XA_WIDTH), 1)
    mk = mk_ref[...]
    mv = mv_ref[...]
    xa = jnp.zeros((tm, XA_WIDTH), F32)
    for hh in range(XA_WIDTH // HEAD64):
        in_head = (lane >= hh * HEAD64) & (lane < (hh + 1) * HEAD64)
        s = _dot_nt(jnp.where(in_head, qn, jnp.zeros_like(qn)), mk)
        p = jnp.exp(s - jnp.max(s, axis=1, keepdims=True))
        o = _dot(p.astype(BF16), mv) / jnp.sum(p, axis=1, keepdims=True)
        xa = jnp.where(in_head, o, xa)
    mw = mix_ref.shape[1]
    acc = _dot(mix_ref[...], wo_ref[0:mw, :]) + _dot(xa.astype(BF16), wo_ref[mw:, :])
    o_ref[...] = h_ref[...] + acc


def _xattn_out(h, mix, proj, xq_col_block, memkv, q_gain_row, w_out, seq, mem_len):
    t, d = h.shape
    tm = min(TM_PROJ, seq)
    per_b = seq // tm
    mw = mix.shape[1]
    return pl.pallas_call(
        _xattn_out_kernel,
        grid=(t // tm,),
        in_specs=[
            pl.BlockSpec((tm, d), lambda i: (i, 0)),
            pl.BlockSpec((tm, mw), lambda i: (i, 0)),
            pl.BlockSpec((tm, XA_WIDTH), lambda i: (i, xq_col_block)),
            pl.BlockSpec((mem_len, XA_WIDTH), lambda i: (i // per_b, 0)),
            pl.BlockSpec((mem_len, XA_WIDTH), lambda i: (i // per_b, 1)),
            pl.BlockSpec((1, XA_WIDTH), lambda i: (0, 0)),
            pl.BlockSpec((XA_WIDTH, XA_WIDTH), lambda i: (0, 0)),
            pl.BlockSpec((mw + XA_WIDTH, d), lambda i: (0, 0)),
        ],
        out_specs=pl.BlockSpec((tm, d), lambda i: (i, 0)),
        out_shape=jax.ShapeDtypeStruct((t, d), F32),
        compiler_params=_cparams("parallel"),
        name="xattn_out",
    )(h, mix, proj, memkv, memkv, q_gain_row, _head_sum_matrix(XA_WIDTH, HEAD64), w_out)


def _ffn_kernel(x_ref, g_ref, wg_ref, wu_ref, wd_ref, o_ref, xn_ref, acc_ref):
    f = pl.program_id(1)

    @pl.when(f == 0)
    def _():
        xn_ref[...] = _rms_rows(x_ref[...], g_ref[...]).astype(BF16)
        acc_ref[...] = jnp.zeros_like(acc_ref)

    xn = xn_ref[...]
    g = _dot(xn, wg_ref[...])
    up = _dot(xn, wu_ref[...])
    act = (g * _sigmoid(g) * up).astype(BF16)
    acc_ref[...] += _dot(act, wd_ref[...])

    @pl.when(f == pl.num_programs(1) - 1)
    def _():
        o_ref[...] = x_ref[...] + acc_ref[...]


def _ffn(h, gain, wg, wu, wd):
    t, d = h.shape
    ff = wg.shape[1]
    tm = min(TM_FFN, t)
    tf = TF_FFN
    return pl.pallas_call(
        _ffn_kernel,
        grid=(t // tm, ff // tf),
        in_specs=[
            pl.BlockSpec((tm, d), lambda i, f: (i, 0)),
            pl.BlockSpec((1, d), lambda i, f: (0, 0)),
            pl.BlockSpec((d, tf), lambda i, f: (0, f)),
            pl.BlockSpec((d, tf), lambda i, f: (0, f)),
            pl.BlockSpec((tf, d), lambda i, f: (f, 0)),
        ],
        out_specs=pl.BlockSpec((tm, d), lambda i, f: (i, 0)),
        out_shape=jax.ShapeDtypeStruct((t, d), F32),
        scratch_shapes=[pltpu.VMEM((tm, d), BF16), pltpu.VMEM((tm, d), F32)],
        compiler_params=_cparams("parallel", "arbitrary"),
        name="ffn_dense",
    )(h, gain.reshape(1, d), wg, wu, wd)


def _dn_prep_kernel(x_ref, halo_ref, ab_ref, cw_ref, alog_ref, dtb_ref, tril_ref, ones_ref,
                    qkv_ref, gcb_ref, gl_ref, *, per_b):
    i = pl.program_id(0)
    tm = x_ref.shape[0]
    x = x_ref[...].astype(F32)
    keep = (i % per_b != 0).astype(F32)
    halo = halo_ref[...].astype(F32)[-8:, :] * keep
    row8 = lax.broadcasted_iota(I32, (8, x.shape[1]), 0)
    cw = cw_ref[...]
    acc = x * cw[DN_CONV - 1:DN_CONV, :]
    for s in range(1, DN_CONV):
        xs = pltpu.roll(x, s, 0)
        hs = pltpu.roll(halo, s, 0)
        first = jnp.where(row8 < s, hs, xs[0:8, :])
        xs = jnp.concatenate([first, xs[8:, :]], axis=0)
        acc = acc + xs * cw[DN_CONV - 1 - s:DN_CONV - s, :]
    y = acc * _sigmoid(acc)
    for c in range(3 * DN_HEADS):
        cols = slice(c * DN_HEAD, (c + 1) * DN_HEAD)
        blk = y[:, cols]
        if c < 2 * DN_HEADS:
            blk = blk * lax.rsqrt(jnp.sum(blk * blk, axis=-1, keepdims=True) + EPS)
            if c < DN_HEADS:
                blk = blk * (DN_HEAD ** -0.5)
        qkv_ref[:, cols] = blk.astype(qkv_ref.dtype)

    ab = ab_ref[...]
    lane = lax.broadcasted_iota(I32, ab.shape, 1)
    g = -jnp.exp(alog_ref[...]) * _softplus(ab + dtb_ref[...])
    g = jnp.where(lane < DN_HEADS, g, 0.0)
    beta = _sigmoid(ab)
    g_hi = g.astype(BF16)
    g_mid = (g - g_hi.astype(F32)).astype(BF16)
    g_lo = (g - g_hi.astype(F32) - g_mid.astype(F32)).astype(BF16)
    tril = tril_ref[...]
    ones = ones_ref[...]
    gc = _dot(tril, g_hi) + _dot(tril, g_mid) + _dot(tril, g_lo)
    gl = _dot(ones, g_hi) + _dot(ones, g_mid) + _dot(ones, g_lo)
    gcb_ref[...] = jnp.where(lane < DN_HEADS, gc, beta)
    gl_ref[...] = gl


def _dn_prep(proj, ab, conv_w, alog_row, dtb_row, seq):
    t = proj.shape[0]
    tm = min(TM_PROJ, seq)
    per_b = seq // tm
    w3 = 3 * DN_WIDTH
    hb = 16
    r = jnp.arange(tm)[:, None]
    c = jnp.arange(tm)[None, :]
    same = (r // DN_CHUNK) == (c // DN_CHUNK)
    tril = (same & (c <= r)).astype(BF16)
    ones = same.astype(BF16)
    return pl.pallas_call(
        functools.partial(_dn_prep_kernel, per_b=per_b),
        grid=(t // tm,),
        in_specs=[
            pl.BlockSpec((tm, w3), lambda i: (i, 0)),
            pl.BlockSpec((hb, w3), lambda i: (jnp.maximum(i * (tm // hb) - 1, 0), 0)),
            pl.BlockSpec((tm, LANES), lambda i: (i, 0)),
            pl.BlockSpec((DN_CONV, w3), lambda i: (0, 0)),
            pl.BlockSpec((1, LANES), lambda i: (0, 0)),
            pl.BlockSpec((1, LANES), lambda i: (0, 0)),
            pl.BlockSpec((tm, tm), lambda i: (0, 0)),
            pl.BlockSpec((tm, tm), lambda i: (0, 0)),
        ],
        out_specs=[
            pl.BlockSpec((tm, w3), lambda i: (i, 0)),
            pl.BlockSpec((tm, LANES), lambda i: (i, 0)),
            pl.BlockSpec((tm, LANES), lambda i: (i, 0)),
        ],
        out_shape=[
            jax.ShapeDtypeStruct((t, w3), BF16),
            jax.ShapeDtypeStruct((t, LANES), F32),
            jax.ShapeDtypeStruct((t, LANES), F32),
        ],
        compiler_params=_cparams("parallel"),
        name="dn_prep",
    )(proj, proj, ab, conv_w, alog_row, dtb_row, tril, ones)


def _delta_kernel(q_ref, k_ref, v_ref, gate_ref, gcb_ref, gl_ref, gct_ref, og_ref, o_ref, s_ref):
    hh = pl.program_id(1)
    rows = q_ref.shape[0]
    n_chunks = rows // DN_CHUNK

    @pl.when(pl.program_id(2) == 0)
    def _():
        s_ref[...] = jnp.zeros_like(s_ref)

    lane = lax.broadcasted_iota(I32, (rows, LANES), 1)
    gcb = gcb_ref[...]
    gc_col = jnp.sum(jnp.where(lane == hh, gcb, 0.0), axis=1, keepdims=True)
    beta_col = jnp.sum(jnp.where(lane == hh + DN_HEADS, gcb, 0.0), axis=1, keepdims=True)
    gl_col = jnp.sum(jnp.where(lane == hh, gl_ref[...], 0.0), axis=1, keepdims=True)
    gc_row = gct_ref[0]

    ri = lax.broadcasted_iota(I32, (rows, rows), 0)
    ci = lax.broadcasted_iota(I32, (rows, rows), 1)
    same = (ri // DN_CHUNK) == (ci // DN_CHUNK)
    tri = same & (ci <= ri)
    strict = same & (ci < ri)
    decay = jnp.exp(jnp.where(tri, gc_col - gc_row, -1e30))

    k = k_ref[...]
    q = q_ref[...]
    kf = k.astype(F32)
    eg = jnp.exp(gc_col)
    kb = kf * beta_col
    vb = v_ref[...].astype(F32) * beta_col
    kk = _dot_nt(kb.astype(BF16), k) * decay
    nmat = jnp.where(strict, -kk, 0.0)
    eye = (ri == ci).astype(F32)
    inv = eye + nmat
    npow = nmat
    for _ in range(5):
        nb = npow.astype(BF16)
        npow = _dot(nb, nb)
        inv = inv + _dot(npow.astype(BF16), inv.astype(BF16))
    rhs = jnp.concatenate([vb, kb * eg], axis=1).astype(BF16)
    uw = _dot(inv.astype(BF16), rhs)
    u = uw[:, :DN_HEAD]
    w = uw[:, DN_HEAD:]
    qk = _dot_nt(q, k) * decay
    q_dec = q.astype(F32) * eg
    k_dec = kf * jnp.exp(gl_col - gc_col)

    state = s_ref[...]
    outs = []
    for c in range(n_chunks):
        rs = slice(c * DN_CHUNK, (c + 1) * DN_CHUNK)
        sb = state.astype(BF16)
        ws = _dot(jnp.concatenate([w[rs], q_dec[rs]], axis=0).astype(BF16), sb)
        v_new = (u[rs] - ws[:DN_CHUNK]).astype(BF16)
        outs.append(ws[DN_CHUNK:] + _dot(qk[rs, rs].astype(BF16), v_new))
        state = state * jnp.exp(gl_col[c * DN_CHUNK:c * DN_CHUNK + 1, :]) + _dot_tn(k_dec[rs].astype(BF16), v_new)
    s_ref[...] = state
    o = jnp.concatenate(outs, axis=0)
    gate = gate_ref[...].astype(F32)
    o_ref[...] = (_rms_rows(o, og_ref[...]) * (gate * _sigmoid(gate))).astype(o_ref.dtype)


def _delta_rule(qkvn, proj, gcb, gl, gct, o_gain_row, batch, seq):
    rows = min(DN_ROWS, seq)
    steps = seq // rows
    t = batch * seq
    gate_blk0 = 3 * DN_HEADS
    return pl.pallas_call(
        _delta_kernel,
        grid=(batch, DN_HEADS, steps),
        in_specs=[
            pl.BlockSpec((rows, DN_HEAD), lambda b, h, s: (b * steps + s, h)),
            pl.BlockSpec((rows, DN_HEAD), lambda b, h, s: (b * steps + s, DN_HEADS + h)),
            pl.BlockSpec((rows, DN_HEAD), lambda b, h, s: (b * steps + s, 2 * DN_HEADS + h)),
            pl.BlockSpec((rows, DN_HEAD), lambda b, h, s: (b * steps + s, gate_blk0 + h)),
            pl.BlockSpec((rows, LANES), lambda b, h, s: (b * steps + s, 0)),
            pl.BlockSpec((rows, LANES), lambda b, h, s: (b * steps + s, 0)),
            pl.BlockSpec((1, 1, rows), lambda b, h, s: (h, 0, b * steps + s)),
            pl.BlockSpec((1, DN_HEAD), lambda b, h, s: (0, 0)),
        ],
        out_specs=pl.BlockSpec((rows, DN_HEAD), lambda b, h, s: (b * steps + s, h)),
        out_shape=jax.ShapeDtypeStruct((t, DN_WIDTH), BF16),
        scratch_shapes=[pltpu.VMEM((DN_HEAD, DN_HEAD), F32)],
        compiler_params=_cparams("parallel", "parallel", "arbitrary"),
        name="delta_rule",
    )(qkvn, qkvn, qkvn, proj, gcb, gl, gct, o_gain_row)


def _router_kernel(h_ref, g_ref, rt_ref, us_ref, hn_ref, meta_ref, wts_ref, cstart_ref, carry_ref):
    i = pl.program_id(0)

    @pl.when(i == 0)
    def _():
        carry_ref[...] = jnp.zeros_like(carry_ref)

    hn = _rms_rows(h_ref[...], g_ref[...])
    hn_ref[...] = hn.astype(BF16)
    logits = lax.dot_general(rt_ref[...], hn, (((1,), (1,)), ((), ())),
                             precision=lax.Precision.HIGHEST, preferred_element_type=F32)
    sub = lax.broadcasted_iota(I32, logits.shape, 0)
    m1 = jnp.max(logits, axis=0, keepdims=True)
    i1 = jnp.min(jnp.where(logits == m1, sub, N_EXPERTS), axis=0, keepdims=True)
    rest = jnp.where(sub == i1, -jnp.inf, logits)
    m2 = jnp.max(rest, axis=0, keepdims=True)
    i2 = jnp.min(jnp.where(rest == m2, sub, N_EXPERTS), axis=0, keepdims=True)
    e21 = jnp.exp(m2 - m1)
    w1 = 1.0 / (1.0 + e21)
    w2 = e21 / (1.0 + e21)
    hit1 = sub == i1
    hit2 = sub == i2
    onehot = (hit1 | hit2).astype(F32)
    carry = carry_ref[...]
    cstart_ref[0] = carry
    before = _dot(onehot.astype(BF16), us_ref[...]) + carry[:, 0:1]
    pos1 = jnp.sum(jnp.where(hit1, before, 0.0), axis=0, keepdims=True).astype(I32)
    pos2 = jnp.sum(jnp.where(hit2, before, 0.0), axis=0, keepdims=True).astype(I32)
    carry_ref[...] = carry + jnp.sum(onehot, axis=1, keepdims=True)
    meta_ref[...] = jnp.where(sub == 0, i1, jnp.where(sub == 1, i2, jnp.where(sub == 2, pos1, jnp.where(sub == 3, pos2, 0))))
    wts_ref[...] = jnp.where(sub == 0, w1, jnp.where(sub == 1, w2, 0.0))


def _router(h, gain, router_t):
    t, d = h.shape
    tm = min(MOE_BLK, t)
    nwin = t // tm
    r = jnp.arange(tm)[:, None]
    c = jnp.arange(tm)[None, :]
    us = (r < c).astype(BF16)
    return pl.pallas_call(
        _router_kernel,
        grid=(nwin,),
        in_specs=[
            pl.BlockSpec((tm, d), lambda i: (i, 0)),
            pl.BlockSpec((1, d), lambda i: (0, 0)),
            pl.BlockSpec((N_EXPERTS, d), lambda i: (0, 0)),
            pl.BlockSpec((tm, tm), lambda i: (0, 0)),
        ],
        out_specs=[
            pl.BlockSpec((tm, d), lambda i: (i, 0)),
            pl.BlockSpec((N_EXPERTS, tm), lambda i: (0, i)),
            pl.BlockSpec((N_EXPERTS, tm), lambda i: (0, i)),
            pl.BlockSpec((1, N_EXPERTS, LANES), lambda i: (i, 0, 0)),
        ],
        out_shape=[
            jax.ShapeDtypeStruct((t, d), BF16),
            jax.ShapeDtypeStruct((N_EXPERTS, t), I32),
            jax.ShapeDtypeStruct((N_EXPERTS, t), F32),
            jax.ShapeDtypeStruct((nwin, N_EXPERTS, LANES), F32),
        ],
        scratch_shapes=[pltpu.VMEM((N_EXPERTS, LANES), F32)],
        compiler_params=_cparams("arbitrary"),
        name="moe_router",
    )(h, gain.reshape(1, d), router_t, us)


def _dispatch_kernel(blk_ref, win_ref, flag_ref, hn_ref, slots_ref, wts_ref, xs_ref, ws_ref):
    l = pl.program_id(0)
    flag = flag_ref[l]
    nslot = xs_ref.shape[0]

    @pl.when((flag & 2) != 0)
    def _():
        xs_ref[...] = jnp.zeros_like(xs_ref)
        ws_ref[...] = jnp.zeros_like(ws_ref)

    @pl.when((flag & 1) != 0)
    def _():
        ntok = hn_ref.shape[0]
        slot = blk_ref[l] * nslot + lax.broadcasted_iota(I32, (nslot, ntok), 0)
        hit1 = slots_ref[0:1, :] == slot
        hit2 = slots_ref[1:2, :] == slot
        sel = jnp.where(hit1 | hit2, 1.0, 0.0).astype(BF16)
        xs_ref[...] += _dot(sel, hn_ref[...]).astype(xs_ref.dtype)
        wsel = jnp.where(hit1, wts_ref[0:1, :], 0.0) + jnp.where(hit2, wts_ref[1:2, :], 0.0)
        ws_ref[...] += jnp.sum(wsel, axis=1, keepdims=True)


def _dispatch(hn, slots_rows, wts, blk, win, flag, cap):
    t, d = hn.shape
    b = MOE_BLK
    n_pairs = blk.shape[0]
    return pl.pallas_call(
        _dispatch_kernel,
        grid_spec=pltpu.PrefetchScalarGridSpec(
            num_scalar_prefetch=3,
            grid=(n_pairs,),
            in_specs=[
                pl.BlockSpec((b, d), lambda l, blk, win, flag: (win[l], 0)),
                pl.BlockSpec((N_EXPERTS, b), lambda l, blk, win, flag: (0, win[l])),
                pl.BlockSpec((N_EXPERTS, b), lambda l, blk, win, flag: (0, win[l])),
            ],
            out_specs=[
                pl.BlockSpec((b, d), lambda l, blk, win, flag: (blk[l], 0)),
                pl.BlockSpec((b, LANES), lambda l, blk, win, flag: (blk[l], 0)),
            ],
        ),
        out_shape=[jax.ShapeDtypeStruct((cap, d), BF16), jax.ShapeDtypeStruct((cap, LANES), F32)],
        compiler_params=_cparams("arbitrary"),
        name="moe_dispatch",
    )(blk, win, flag, hn, slots_rows, wts)


def _experts_kernel(be_ref, src_ref, used_ref, xs_ref, ws_ref, wg_ref, wu_ref, wd_ref, y_ref, acc_ref):
    b = pl.program_id(0)
    f = pl.program_id(1)
    used = used_ref[b] != 0

    @pl.when(f == 0)
    def _():
        acc_ref[...] = jnp.zeros_like(acc_ref)

    @pl.when(used)
    def _():
        xs = xs_ref[...]
        g = _dot(xs, wg_ref[...])
        up = _dot(xs, wu_ref[...])
        act = (g * _sigmoid(g) * up).astype(BF16)
        acc_ref[...] += _dot(act, wd_ref[...])

    @pl.when(f == pl.num_programs(1) - 1)
    def _():
        y_ref[...] = jnp.where(used, acc_ref[...] * ws_ref[:, 0:1], 0.0).astype(y_ref.dtype)


def _experts(xs, ws, wg, wu, wd, blk_expert, blk_src, blk_used):
    cap, d = xs.shape
    ff = wg.shape[2]
    b = MOE_BLK
    tf = TF_FFN
    return pl.pallas_call(
        _experts_kernel,
        grid_spec=pltpu.PrefetchScalarGridSpec(
            num_scalar_prefetch=3,
            grid=(cap // b, ff // tf),
            in_specs=[
                pl.BlockSpec((b, d), lambda i, f, be, src, used: (src[i], 0)),
                pl.BlockSpec((b, LANES), lambda i, f, be, src, used: (src[i], 0)),
                pl.BlockSpec((None, d, tf), lambda i, f, be, src, used: (be[i], 0, f)),
                pl.BlockSpec((None, d, tf), lambda i, f, be, src, used: (be[i], 0, f)),
                pl.BlockSpec((None, tf, d), lambda i, f, be, src, used: (be[i], f, 0)),
            ],
            out_specs=pl.BlockSpec((b, d), lambda i, f, be, src, used: (i, 0)),
            scratch_shapes=[pltpu.VMEM((b, d), F32)],
        ),
        out_shape=jax.ShapeDtypeStruct((cap, d), BF16),
        compiler_params=_cparams("parallel", "arbitrary"),
        name="moe_experts",
    )(blk_expert, blk_src, blk_used, xs, ws, wg, wu, wd)


def _combine_kernel(win_ref, blk_ref, flag_ref, h_ref, y_ref, slots_ref, o_ref):
    l = pl.program_id(0)
    flag = flag_ref[l]

    @pl.when((flag & 2) != 0)
    def _():
        o_ref[...] = h_ref[...]

    @pl.when((flag & 1) != 0)
    def _():
        ntok = h_ref.shape[0]
        nslot = y_ref.shape[0]
        slot = blk_ref[l] * nslot + lax.broadcasted_iota(I32, (ntok, nslot), 1)
        hit = (slots_ref[:, 0:1] == slot) | (slots_ref[:, 1:2] == slot)
        o_ref[...] += _dot(jnp.where(hit, 1.0, 0.0).astype(BF16), y_ref[...])


def _combine(h, y, slots_cols, win, blk, flag):
    t, d = h.shape
    b = MOE_BLK
    n_pairs = win.shape[0]
    return pl.pallas_call(
        _combine_kernel,
        grid_spec=pltpu.PrefetchScalarGridSpec(
            num_scalar_prefetch=3,
            grid=(n_pairs,),
            in_specs=[
                pl.BlockSpec((b, d), lambda l, win, blk, flag: (win[l], 0)),
                pl.BlockSpec((b, d), lambda l, win, blk, flag: (blk[l], 0)),
                pl.BlockSpec((b, N_EXPERTS), lambda l, win, blk, flag: (win[l], 0)),
            ],
            out_specs=pl.BlockSpec((b, d), lambda l, win, blk, flag: (win[l], 0)),
        ),
        out_shape=jax.ShapeDtypeStruct((t, d), F32),
        compiler_params=_cparams("arbitrary"),
        name="moe_combine",
    )(win, blk, flag, h, y, slots_cols)


def _pair_list(first_blk, n_blk, order_major_is_expert, n_pairs):
    n_e, n_w = first_blk.shape
    if order_major_is_expert:
        fb, nb = first_blk.reshape(-1), n_blk.reshape(-1)
        win_of = jnp.tile(jnp.arange(n_w, dtype=I32), n_e)
    else:
        fb, nb = first_blk.T.reshape(-1), n_blk.T.reshape(-1)
        win_of = jnp.repeat(jnp.arange(n_w, dtype=I32), n_e)
    end = jnp.cumsum(nb)
    total = end[-1]
    l = jnp.arange(n_pairs, dtype=I32)
    lc = jnp.minimum(l, total - 1)
    p = jnp.searchsorted(end, lc, side="right").astype(I32)
    blk = fb[p] + (lc - (end[p] - nb[p]))
    win = win_of[p]
    valid = l < total
    return blk.astype(I32), win.astype(I32), valid


def _moe(h, gain, router, wg, wu, wd):
    t, d = h.shape
    b = MOE_BLK
    nwin = t // b
    nblk = (2 * t) // b + N_EXPERTS
    cap = nblk * b
    n_pairs = nblk + N_EXPERTS * nwin

    hn, meta, wts, cstart = _router(h, gain, router.T.astype(F32))
    e1, e2, pos1, pos2 = meta[0], meta[1], meta[2], meta[3]

    cstart = cstart[:, :, 0].astype(I32).T
    eid = jnp.arange(N_EXPERTS, dtype=I32)[:, None]
    hit1 = e1[None, :] == eid
    hit2 = e2[None, :] == eid
    counts = jnp.sum((hit1 | hit2).astype(I32), axis=1)
    padded = (counts + b - 1) // b * b
    gend = jnp.cumsum(padded)
    gstart = gend - padded
    slot1 = jnp.sum(jnp.where(hit1, gstart[:, None], 0), axis=0) + pos1
    slot2 = jnp.sum(jnp.where(hit2, gstart[:, None], 0), axis=0) + pos2
    slots_rows = jnp.zeros((N_EXPERTS, t), I32).at[0].set(slot1).at[1].set(slot2)
    slots_cols = slots_rows.T

    cend = jnp.concatenate([cstart[:, 1:], counts[:, None]], axis=1)
    lo = gstart[:, None] + cstart
    hi = gstart[:, None] + cend
    first_blk = lo // b
    n_blk = jnp.where(cend > cstart, (hi - 1) // b - first_blk + 1, 0)

    blk_d, win_d, valid_d = _pair_list(first_blk, n_blk, True, n_pairs)
    new_d = jnp.concatenate([jnp.ones((1,), bool), blk_d[1:] != blk_d[:-1]])
    flag_d = valid_d.astype(I32) + 2 * (new_d & valid_d).astype(I32)
    blk_c, win_c, valid_c = _pair_list(first_blk, n_blk, False, n_pairs)
    new_c = jnp.concatenate([jnp.ones((1,), bool), win_c[1:] != win_c[:-1]])
    flag_c = valid_c.astype(I32) + 2 * (new_c & valid_c).astype(I32)

    n_used = gend[-1] // b
    bidx = jnp.arange(nblk, dtype=I32)
    blk_used = (bidx < n_used).astype(I32)
    blk_src = jnp.minimum(bidx, n_used - 1)
    blk_expert = jnp.minimum(jnp.searchsorted(gend, blk_src * b, side="right"), N_EXPERTS - 1).astype(I32)

    xs, ws = _dispatch(hn, slots_rows, wts, blk_d, win_d, flag_d, cap)
    y = _experts(xs, ws, wg, wu, wd, blk_expert, blk_src, blk_used)
    return _combine(h, y, slots_cols, win_c, blk_c, flag_c)


def _tile_gain(gain, reps, scale=1.0):
    return (jnp.tile(gain.astype(F32), reps) * scale).reshape(1, -1)


def _mem_kv(mem2d, gain, w, k_gain):
    return _norm_proj(mem2d, gain, w.astype(BF16), _tile_gain(k_gain, XA_WIDTH // HEAD64), XA_WIDTH, 2 * XA_WIDTH)[0]


def kernel(x, mem, mix_norm, ffn_norm, mem_norm, w_mem_kv, xa_q_norm, xa_k_norm, w_out, sb_w_in, sb_q_norm, sb_k_norm, dn_w_in, dn_conv, dn_a_log, dn_dt_bias, dn_o_norm, ffd_w_gate, ffd_w_up, ffd_w_down, moe_router, moe_w_gate, moe_w_up, moe_w_down):
    batch, seq, d = x.shape
    mem_len = mem.shape[1]
    t = batch * seq
    h = x.reshape(t, d)
    mem2d = mem.reshape(batch * mem_len, d)
    scale64 = HEAD64 ** -0.5

    qk_gain = jnp.concatenate([_tile_gain(sb_q_norm[0], SB_HEADS, scale64), _tile_gain(sb_k_norm[0], SB_HEADS)], axis=1)
    (proj0,) = _norm_proj(h, mix_norm[0], sb_w_in[0].astype(BF16), qk_gain, 2 * SB_WIDTH, 3 * SB_WIDTH + XA_WIDTH)
    mix0 = _sb_attention(proj0, batch, seq)
    memkv0 = _mem_kv(mem2d, mem_norm[0], w_mem_kv[0], xa_k_norm[0])
    xq_gain0 = _tile_gain(xa_q_norm[0], XA_WIDTH // HEAD64, scale64)
    h = _xattn_out(h, mix0, proj0, (3 * SB_WIDTH) // XA_WIDTH, memkv0, xq_gain0, w_out[0].astype(BF16), seq, mem_len)
    h = _ffn(h, ffn_norm[0], ffd_w_gate[0].astype(BF16), ffd_w_up[0].astype(BF16), ffd_w_down[0].astype(BF16))

    w1 = dn_w_in[0]
    n_main = 4 * DN_WIDTH + XA_WIDTH
    w1 = jnp.concatenate([w1[:, :4 * DN_WIDTH], w1[:, 4 * DN_WIDTH + 2 * DN_HEADS:], w1[:, 4 * DN_WIDTH:4 * DN_WIDTH + 2 * DN_HEADS],
                          jnp.zeros((d, LANES - 2 * DN_HEADS), F32)], axis=1).astype(BF16)
    proj1, ab = _norm_proj(h, mix_norm[1], w1, None, 0, n_main)
    pad = LANES - DN_HEADS
    alog_row = jnp.pad(dn_a_log[0].astype(F32), (0, pad)).reshape(1, LANES)
    dtb_row = jnp.pad(dn_dt_bias[0].astype(F32), (0, pad)).reshape(1, LANES)
    qkvn, gcb, gl = _dn_prep(proj1, ab, dn_conv[0].astype(F32), alog_row, dtb_row, seq)
    gct = jnp.pad(gcb[:, :DN_HEADS].T, ((0, 8 - DN_HEADS), (0, 0))).reshape(8, 1, t)
    mix1 = _delta_rule(qkvn, proj1, gcb, gl, gct, dn_o_norm[0].astype(F32).reshape(1, DN_HEAD), batch, seq)
    memkv1 = _mem_kv(mem2d, mem_norm[1], w_mem_kv[1], xa_k_norm[1])
    xq_gain1 = _tile_gain(xa_q_norm[1], XA_WIDTH // HEAD64, scale64)
    h = _xattn_out(h, mix1, proj1, (4 * DN_WIDTH) // XA_WIDTH, memkv1, xq_gain1, w_out[1].astype(BF16), seq, mem_len)
    h = _moe(h, ffn_norm[1], moe_router[0], moe_w_gate[0].astype(BF16), moe_w_up[0].astype(BF16), moe_w_down[0].astype(BF16))
    return h.reshape(batch, seq, d)
```

```python
import functools

import jax
import jax.numpy as jnp
from jax import lax
from jax.experimental import pallas as pl
from jax.experimental.pallas import tpu as pltpu

F32 = jnp.float32
BF16 = jnp.bfloat16
I32 = jnp.int32

EPS = 1e-6
LOG2E = 1.4426950408889634
D_MODEL = 1024
HEAD64 = 64
SB_HEADS = 12
SB_WIDTH = SB_HEADS * HEAD64
XA_WIDTH = 256
DN_HEADS = 6
DN_HEAD = 128
DN_WIDTH = DN_HEADS * DN_HEAD
DN_CONV = 4
DN_CHUNK = 64
D_FF = 3584
N_EXPERTS = 8

LANES = 128
MXU_DIM = 256
VMEM_LIMIT = 56 * 1024 * 1024

TM_PROJ = 512
TM_FFN = 1024
TF_FFN = 896
TF_MOE = 1792
SB_TQ = 512
SB_TK = 256
DN_ROWS = 256
DN_HEADS_PER_STEP = 6
MOE_BLK = 512


def _cparams(*sem):
    return pltpu.CompilerParams(dimension_semantics=sem, vmem_limit_bytes=VMEM_LIMIT)


def _dot(a, b):
    return jnp.dot(a, b, preferred_element_type=F32)


def _dot_nt(a, b):
    return lax.dot_general(a, b, (((1,), (1,)), ((), ())), preferred_element_type=F32)


def _dot_tn(a, b):
    return lax.dot_general(a, b, (((0,), (0,)), ((), ())), preferred_element_type=F32)


def _rms_rows(x, gain_row):
    ms = jnp.mean(x * x, axis=-1, keepdims=True)
    return x * lax.rsqrt(ms + EPS) * gain_row


def _softplus(x):
    return jnp.maximum(x, 0.0) + jnp.log1p(jnp.exp(-jnp.abs(x)))


def _sigmoid(x):
    return 1.0 / (1.0 + jnp.exp(-x))


def _head_sum_matrix(width, head):
    r = jnp.arange(width)[:, None] // head
    c = jnp.arange(width)[None, :] // head
    return (r == c).astype(BF16)


def _norm_proj_kernel(x_ref, g_ref, w_ref, hg_ref, hm_ref, *o_refs, n_norm_cols, n_main_cols):
    xn = _rms_rows(x_ref[...], g_ref[...]).astype(BF16)
    proj = _dot(xn, w_ref[...])
    o_ref = o_refs[0]
    for c in range(n_norm_cols // MXU_DIM):
        cols = slice(c * MXU_DIM, (c + 1) * MXU_DIM)
        blk = proj[:, cols]
        ss = _dot((blk * blk).astype(BF16), hm_ref[...])
        o_ref[:, cols] = (blk * lax.rsqrt(ss * (1.0 / HEAD64) + EPS) * hg_ref[:, cols]).astype(o_ref.dtype)
    if n_main_cols > n_norm_cols:
        o_ref[:, n_norm_cols:n_main_cols] = proj[:, n_norm_cols:n_main_cols].astype(o_ref.dtype)
    if len(o_refs) > 1:
        o_refs[1][...] = proj[:, n_main_cols:]


def _norm_proj(x, gain, w, head_gain, n_norm_cols, n_main_cols):
    t, d = x.shape
    n = w.shape[1]
    tm = min(TM_PROJ, t)
    out_shape = [jax.ShapeDtypeStruct((t, n_main_cols), BF16)]
    out_specs = [pl.BlockSpec((tm, n_main_cols), lambda i: (i, 0))]
    if n > n_main_cols:
        out_shape.append(jax.ShapeDtypeStruct((t, n - n_main_cols), F32))
        out_specs.append(pl.BlockSpec((tm, n - n_main_cols), lambda i: (i, 0)))
    hg = jnp.zeros((1, max(n_norm_cols, MXU_DIM)), F32) if head_gain is None else head_gain
    res = pl.pallas_call(
        functools.partial(_norm_proj_kernel, n_norm_cols=n_norm_cols, n_main_cols=n_main_cols),
        grid=(t // tm,),
        in_specs=[
            pl.BlockSpec((tm, d), lambda i: (i, 0)),
            pl.BlockSpec((1, d), lambda i: (0, 0)),
            pl.BlockSpec((d, n), lambda i: (0, 0)),
            pl.BlockSpec(hg.shape, lambda i: (0, 0)),
            pl.BlockSpec((MXU_DIM, MXU_DIM), lambda i: (0, 0)),
        ],
        out_specs=out_specs,
        out_shape=out_shape,
        compiler_params=_cparams("parallel"),
        name="norm_proj",
    )(x, gain.reshape(1, d), w, hg, _head_sum_matrix(MXU_DIM, HEAD64))
    return res


def _sb_attn_kernel(qt_ref, k_ref, v_ref, nut_ref, o_ref):
    i = pl.program_id(2)
    tq = qt_ref.shape[1]
    tk = nut_ref.shape[0]
    qt = qt_ref[...]
    sub = lax.broadcasted_iota(I32, qt.shape, 0)
    zero = jnp.zeros_like(qt)
    qqt = jnp.concatenate([jnp.where(sub < HEAD64, qt, zero), jnp.where(sub >= HEAD64, qt, zero)], axis=1)
    nut = nut_ref[...]

    def tile_pair(p, carry, masked):
        offs = [pl.multiple_of((2 * p + 1) * tk, tk), pl.multiple_of(2 * p * tk, tk)]
        zs = [_dot(k_ref[pl.ds(off, tk), :], qqt) for off in offs]
        log_b, spb, causal, within, pv = [], [], [], [], []
        for n, off in enumerate(offs):
            z = zs[n]
            sp = jnp.maximum(z, jnp.log2(1.0 + jnp.exp2(jnp.minimum(z, 126.0))))
            log_b.append(z - sp)
            if masked:
                key = off + lax.broadcasted_iota(I32, z.shape, 0)
                qi = lax.broadcasted_iota(I32, z.shape, 1)
                causal.append(key < i * tq + jnp.where(qi >= tq, qi - tq, qi))
                sp = jnp.where(causal[n], sp, 0.0)
            spb.append(sp.astype(BF16))
            within.append(_dot(nut, spb[n]))
        for n, off in enumerate(offs):
            arg = log_b[n] + within[n] + carry
            if masked:
                arg = jnp.where(causal[n], arg, -1e30)
            pv.append(_dot_tn(v_ref[pl.ds(off, tk), :], jnp.exp2(arg).astype(BF16)))
            carry = carry + within[n][0:1, :] - spb[n][0:1, :].astype(F32)
        return pv[0] + pv[1], carry

    last = ((i + 1) * tq - 1) // (2 * tk)
    init = tile_pair(last, jnp.zeros((1, 2 * tq), F32), True)

    def body(jj, st):
        pv, carry = tile_pair(last - 1 - jj, st[1], False)
        return st[0] + pv, carry

    acc, _ = lax.fori_loop(0, last, body, init)
    lane = lax.broadcasted_iota(I32, (tq, LANES), 1)
    o_ref[...] = jnp.where(lane < HEAD64, acc[:, :tq].T, acc[:, tq:].T).astype(o_ref.dtype)


def _sb_attention(proj, batch, seq):
    tq = min(SB_TQ, seq)
    nq = seq // tq
    n_pairs = SB_WIDTH // LANES
    qt = proj[:, :SB_WIDTH].T
    tk = min(SB_TK, seq // 2)
    s_row = jnp.arange(tk)[:, None]
    j_col = jnp.arange(tk)[None, :]
    nut = -(j_col > s_row).astype(BF16)
    return pl.pallas_call(
        _sb_attn_kernel,
        grid=(batch, n_pairs, nq),
        in_specs=[
            pl.BlockSpec((LANES, tq), lambda b, p, i: (p, b * nq + i)),
            pl.BlockSpec((seq, LANES), lambda b, p, i: (b, n_pairs + p)),
            pl.BlockSpec((seq, LANES), lambda b, p, i: (b, 2 * n_pairs + p)),
            pl.BlockSpec((tk, tk), lambda b, p, i: (0, 0)),
        ],
        out_specs=pl.BlockSpec((tq, LANES), lambda b, p, i: (b * nq + i, p)),
        out_shape=jax.ShapeDtypeStruct((batch * seq, SB_WIDTH), BF16),
        compiler_params=_cparams("parallel", "parallel", "arbitrary"),
        name="sb_attention",
    )(qt, proj, proj, nut)


def _xattn_out_kernel(h_ref, mix_ref, xq_ref, mk_ref, mv_ref, qg_ref, hm_ref, wo_ref, o_ref):
    tm = h_ref.shape[0]
    xq = xq_ref[...].astype(F32)
    ss = _dot((xq * xq).astype(BF16), hm_ref[...])
    qn = (xq * lax.rsqrt(ss * (1.0 / HEAD64) + EPS) * qg_ref[...]).astype(BF16)
    lane = lax.broadcasted_iota(I32, (tm, XA_WIDTH), 1)
    mk = mk_ref[...]
    mv = mv_ref[...]
    xa = jnp.zeros((tm, XA_WIDTH), F32)
    for hh in range(XA_WIDTH // HEAD64):
        in_head = (lane >= hh * HEAD64) & (lane < (hh + 1) * HEAD64)
        s = _dot_nt(jnp.where(in_head, qn, jnp.zeros_like(qn)), mk)
        p = jnp.exp(s - jnp.max(s, axis=1, keepdims=True))
        o = _dot(p.astype(BF16), mv) / jnp.sum(p, axis=1, keepdims=True)
        xa = jnp.where(in_head, o, xa)
    mw = mix_ref.shape[1]
    acc = _dot(mix_ref[...], wo_ref[0:mw, :]) + _dot(xa.astype(BF16), wo_ref[mw:, :])
    o_ref[...] = h_ref[...] + acc


def _xattn_out(h, mix, proj, xq_col_block, memkv, q_gain_row, w_out, seq, mem_len):
    t, d = h.shape
    tm = min(TM_PROJ, seq)
    per_b = seq // tm
    mw = mix.shape[1]
    return pl.pallas_call(
        _xattn_out_kernel,
        grid=(t // tm,),
        in_specs=[
            pl.BlockSpec((tm, d), lambda i: (i, 0)),
            pl.BlockSpec((tm, mw), lambda i: (i, 0)),
            pl.BlockSpec((tm, XA_WIDTH), lambda i: (i, xq_col_block)),
            pl.BlockSpec((mem_len, XA_WIDTH), lambda i: (i // per_b, 0)),
            pl.BlockSpec((mem_len, XA_WIDTH), lambda i: (i // per_b, 1)),
            pl.BlockSpec((1, XA_WIDTH), lambda i: (0, 0)),
            pl.BlockSpec((XA_WIDTH, XA_WIDTH), lambda i: (0, 0)),
            pl.BlockSpec((mw + XA_WIDTH, d), lambda i: (0, 0)),
        ],
        out_specs=pl.BlockSpec((tm, d), lambda i: (i, 0)),
        out_shape=jax.ShapeDtypeStruct((t, d), F32),
        compiler_params=_cparams("parallel"),
        name="xattn_out",
    )(h, mix, proj, memkv, memkv, q_gain_row, _head_sum_matrix(XA_WIDTH, HEAD64), w_out)


def _ffn_kernel(x_ref, g_ref, wg_ref, wu_ref, wd_ref, o_ref, xn_ref, acc_ref):
    f = pl.program_id(1)

    @pl.when(f == 0)
    def _():
        xn_ref[...] = _rms_rows(x_ref[...], g_ref[...]).astype(BF16)
        acc_ref[...] = jnp.zeros_like(acc_ref)

    xn = xn_ref[...]
    g = _dot(xn, wg_ref[...])
    up = _dot(xn, wu_ref[...])
    act = (g * _sigmoid(g) * up).astype(BF16)
    acc_ref[...] += _dot(act, wd_ref[...])

    @pl.when(f == pl.num_programs(1) - 1)
    def _():
        o_ref[...] = x_ref[...] + acc_ref[...]


def _ffn(h, gain, wg, wu, wd):
    t, d = h.shape
    ff = wg.shape[1]
    tm = min(TM_FFN, t)
    tf = TF_FFN
    return pl.pallas_call(
        _ffn_kernel,
        grid=(t // tm, ff // tf),
        in_specs=[
            pl.BlockSpec((tm, d), lambda i, f: (i, 0)),
            pl.BlockSpec((1, d), lambda i, f: (0, 0)),
            pl.BlockSpec((d, tf), lambda i, f: (0, f)),
            pl.BlockSpec((d, tf), lambda i, f: (0, f)),
            pl.BlockSpec((tf, d), lambda i, f: (f, 0)),
        ],
        out_specs=pl.BlockSpec((tm, d), lambda i, f: (i, 0)),
        out_shape=jax.ShapeDtypeStruct((t, d), F32),
        scratch_shapes=[pltpu.VMEM((tm, d), BF16), pltpu.VMEM((tm, d), F32)],
        compiler_params=_cparams("parallel", "arbitrary"),
        name="ffn_dense",
    )(h, gain.reshape(1, d), wg, wu, wd)


def _dn_prep_kernel(x_ref, halo_ref, ab_ref, cw_ref, alog_ref, dtb_ref, tril_ref, ones_ref,
                    qkv_ref, gcb_ref, gl_ref, *, per_b):
    i = pl.program_id(0)
    tm = x_ref.shape[0]
    x = x_ref[...].astype(F32)
    keep = (i % per_b != 0).astype(F32)
    halo = halo_ref[...].astype(F32)[-8:, :] * keep
    row8 = lax.broadcasted_iota(I32, (8, x.shape[1]), 0)
    cw = cw_ref[...]
    acc = x * cw[DN_CONV - 1:DN_CONV, :]
    for s in range(1, DN_CONV):
        xs = pltpu.roll(x, s, 0)
        hs = pltpu.roll(halo, s, 0)
        first = jnp.where(row8 < s, hs, xs[0:8, :])
        xs = jnp.concatenate([first, xs[8:, :]], axis=0)
        acc = acc + xs * cw[DN_CONV - 1 - s:DN_CONV - s, :]
    y = acc * _sigmoid(acc)
    for c in range(3 * DN_HEADS):
        cols = slice(c * DN_HEAD, (c + 1) * DN_HEAD)
        blk = y[:, cols]
        if c < 2 * DN_HEADS:
            blk = blk * lax.rsqrt(jnp.sum(blk * blk, axis=-1, keepdims=True) + EPS)
            if c < DN_HEADS:
                blk = blk * (DN_HEAD ** -0.5)
        qkv_ref[:, cols] = blk.astype(qkv_ref.dtype)

    ab = ab_ref[...]
    lane = lax.broadcasted_iota(I32, ab.shape, 1)
    g = -jnp.exp(alog_ref[...]) * _softplus(ab + dtb_ref[...])
    g = jnp.where(lane < DN_HEADS, g, 0.0)
    beta = _sigmoid(ab)
    g_hi = g.astype(BF16)
    g_mid = (g - g_hi.astype(F32)).astype(BF16)
    g_lo = (g - g_hi.astype(F32) - g_mid.astype(F32)).astype(BF16)
    tril = tril_ref[...]
    ones = ones_ref[...]
    gc = _dot(tril, g_hi) + _dot(tril, g_mid) + _dot(tril, g_lo)
    gl = _dot(ones, g_hi) + _dot(ones, g_mid) + _dot(ones, g_lo)
    gcb_ref[...] = jnp.where(lane < DN_HEADS, gc, beta)
    gl_ref[...] = gl


def _dn_prep(proj, ab, conv_w, alog_row, dtb_row, seq):
    t = proj.shape[0]
    tm = min(TM_PROJ, seq)
    per_b = seq // tm
    w3 = 3 * DN_WIDTH
    hb = 16
    r = jnp.arange(tm)[:, None]
    c = jnp.arange(tm)[None, :]
    same = (r // DN_CHUNK) == (c // DN_CHUNK)
    tril = (same & (c <= r)).astype(BF16)
    ones = same.astype(BF16)
    return pl.pallas_call(
        functools.partial(_dn_prep_kernel, per_b=per_b),
        grid=(t // tm,),
        in_specs=[
            pl.BlockSpec((tm, w3), lambda i: (i, 0)),
            pl.BlockSpec((hb, w3), lambda i: (jnp.maximum(i * (tm // hb) - 1, 0), 0)),
            pl.BlockSpec((tm, LANES), lambda i: (i, 0)),
            pl.BlockSpec((DN_CONV, w3), lambda i: (0, 0)),
            pl.BlockSpec((1, LANES), lambda i: (0, 0)),
            pl.BlockSpec((1, LANES), lambda i: (0, 0)),
            pl.BlockSpec((tm, tm), lambda i: (0, 0)),
            pl.BlockSpec((tm, tm), lambda i: (0, 0)),
        ],
        out_specs=[
            pl.BlockSpec((tm, w3), lambda i: (i, 0)),
            pl.BlockSpec((tm, LANES), lambda i: (i, 0)),
            pl.BlockSpec((tm, LANES), lambda i: (i, 0)),
        ],
        out_shape=[
            jax.ShapeDtypeStruct((t, w3), BF16),
            jax.ShapeDtypeStruct((t, LANES), F32),
            jax.ShapeDtypeStruct((t, LANES), F32),
        ],
        compiler_params=_cparams("parallel"),
        name="dn_prep",
    )(proj, proj, ab, conv_w, alog_row, dtb_row, tril, ones)


def _delta_kernel(q_ref, k_ref, v_ref, gate_ref, gcb_ref, gl_ref, gct_ref, og_ref, o_ref, s_ref, *, heads):
    h0 = pl.program_id(1) * heads
    rows = q_ref.shape[0]
    n_chunks = rows // DN_CHUNK
    hs = range(heads)

    @pl.when(pl.program_id(2) == 0)
    def _():
        s_ref[...] = jnp.zeros_like(s_ref)

    lane = lax.broadcasted_iota(I32, (rows, LANES), 1)
    gcb = gcb_ref[...]
    glb = gl_ref[...]
    gc_col = [jnp.sum(jnp.where(lane == h0 + h, gcb, 0.0), axis=1, keepdims=True) for h in hs]
    beta_col = [jnp.sum(jnp.where(lane == h0 + h + DN_HEADS, gcb, 0.0), axis=1, keepdims=True) for h in hs]
    gl_col = [jnp.sum(jnp.where(lane == h0 + h, glb, 0.0), axis=1, keepdims=True) for h in hs]

    ri = lax.broadcasted_iota(I32, (rows, rows), 0)
    ci = lax.broadcasted_iota(I32, (rows, rows), 1)
    same = (ri // DN_CHUNK) == (ci // DN_CHUNK)
    tri = same & (ci <= ri)
    strict = same & (ci < ri)
    eye = (ri == ci).astype(F32)
    cols = [slice(h * DN_HEAD, (h + 1) * DN_HEAD) for h in hs]

    decay = [jnp.exp(jnp.where(tri, gc_col[h] - gct_ref[h], -1e30)) for h in hs]
    k = [k_ref[:, cols[h]] for h in hs]
    q = [q_ref[:, cols[h]] for h in hs]
    kf = [k[h].astype(F32) for h in hs]
    eg = [jnp.exp(gc_col[h]) for h in hs]
    kb = [kf[h] * beta_col[h] for h in hs]
    rhs = [jnp.concatenate([v_ref[:, cols[h]].astype(F32) * beta_col[h], kb[h] * eg[h]], axis=1).astype(BF16) for h in hs]
    npow = [jnp.where(strict, -(_dot_nt(kb[h].astype(BF16), k[h]) * decay[h]), 0.0) for h in hs]
    inv = [eye + npow[h] for h in hs]
    for _ in range(5):
        nb = [npow[h].astype(BF16) for h in hs]
        npow = [_dot(nb[h], nb[h]) for h in hs]
        inv = [inv[h] + _dot(npow[h].astype(BF16), inv[h].astype(BF16)) for h in hs]
    uw = [_dot(inv[h].astype(BF16), rhs[h]) for h in hs]
    qk = [(_dot_nt(q[h], k[h]) * decay[h]).astype(BF16) for h in hs]
    w_b = [uw[h][:, DN_HEAD:].astype(BF16) for h in hs]
    q_dec = [(q[h].astype(F32) * eg[h]).astype(BF16) for h in hs]
    k_dec = [(kf[h] * jnp.exp(gl_col[h] - gc_col[h])).astype(BF16) for h in hs]

    state = [s_ref[h] for h in hs]
    outs = [[] for _ in hs]
    for c in range(n_chunks):
        rs = slice(c * DN_CHUNK, (c + 1) * DN_CHUNK)
        sb = [state[h].astype(BF16) for h in hs]
        ws = [_dot(jnp.concatenate([w_b[h][rs], q_dec[h][rs]], axis=0), sb[h]) for h in hs]
        v_new = [(uw[h][rs, :DN_HEAD] - ws[h][:DN_CHUNK]).astype(BF16) for h in hs]
        for h in hs:
            outs[h].append(ws[h][DN_CHUNK:] + _dot(qk[h][rs, rs], v_new[h]))
        state = [state[h] * jnp.exp(gl_col[h][c * DN_CHUNK:c * DN_CHUNK + 1, :]) + _dot_tn(k_dec[h][rs], v_new[h]) for h in hs]
    for h in hs:
        s_ref[h] = state[h]
        o = jnp.concatenate(outs[h], axis=0)
        gate = gate_ref[:, cols[h]].astype(F32)
        o_ref[:, cols[h]] = (_rms_rows(o, og_ref[...]) * (gate * _sigmoid(gate))).astype(o_ref.dtype)


def _delta_rule(qkvn, proj, gcb, gl, gct, o_gain_row, batch, seq):
    rows = min(DN_ROWS, seq)
    steps = seq // rows
    t = batch * seq
    heads = DN_HEADS_PER_STEP
    groups = DN_HEADS // heads
    w = heads * DN_HEAD
    return pl.pallas_call(
        functools.partial(_delta_kernel, heads=heads),
        grid=(batch, groups, steps),
        in_specs=[
            pl.BlockSpec((rows, w), lambda b, g, s: (b * steps + s, g)),
            pl.BlockSpec((rows, w), lambda b, g, s: (b * steps + s, groups + g)),
            pl.BlockSpec((rows, w), lambda b, g, s: (b * steps + s, 2 * groups + g)),
            pl.BlockSpec((rows, w), lambda b, g, s: (b * steps + s, 3 * groups + g)),
            pl.BlockSpec((rows, LANES), lambda b, g, s: (b * steps + s, 0)),
            pl.BlockSpec((rows, LANES), lambda b, g, s: (b * steps + s, 0)),
            pl.BlockSpec((heads, 1, rows), lambda b, g, s: (g, 0, b * steps + s)),
            pl.BlockSpec((1, DN_HEAD), lambda b, g, s: (0, 0)),
        ],
        out_specs=pl.BlockSpec((rows, w), lambda b, g, s: (b * steps + s, g)),
        out_shape=jax.ShapeDtypeStruct((t, DN_WIDTH), BF16),
        scratch_shapes=[pltpu.VMEM((heads, DN_HEAD, DN_HEAD), F32)],
        compiler_params=_cparams("parallel", "parallel", "arbitrary"),
        name="delta_rule",
    )(qkvn, qkvn, qkvn, proj, gcb, gl, gct, o_gain_row)


def _router_kernel(h_ref, g_ref, rt_ref, us_ref, hn_ref, meta_ref, wts_ref, cstart_ref, carry_ref):
    i = pl.program_id(0)

    @pl.when(i == 0)
    def _():
        carry_ref[...] = jnp.zeros_like(carry_ref)

    hn = _rms_rows(h_ref[...], g_ref[...])
    hn_ref[...] = hn.astype(BF16)
    logits = lax.dot_general(rt_ref[...], hn, (((1,), (1,)), ((), ())),
                             precision=lax.Precision.HIGHEST, preferred_element_type=F32)
    sub = lax.broadcasted_iota(I32, logits.shape, 0)
    m1 = jnp.max(logits, axis=0, keepdims=True)
    i1 = jnp.min(jnp.where(logits == m1, sub, N_EXPERTS), axis=0, keepdims=True)
    rest = jnp.where(sub == i1, -jnp.inf, logits)
    m2 = jnp.max(rest, axis=0, keepdims=True)
    i2 = jnp.min(jnp.where(rest == m2, sub, N_EXPERTS), axis=0, keepdims=True)
    e21 = jnp.exp(m2 - m1)
    w1 = 1.0 / (1.0 + e21)
    w2 = e21 / (1.0 + e21)
    hit1 = sub == i1
    hit2 = sub == i2
    onehot = (hit1 | hit2).astype(F32)
    carry = carry_ref[...]
    cstart_ref[0] = carry
    before = _dot(onehot.astype(BF16), us_ref[...]) + carry[:, 0:1]
    pos1 = jnp.sum(jnp.where(hit1, before, 0.0), axis=0, keepdims=True).astype(I32)
    pos2 = jnp.sum(jnp.where(hit2, before, 0.0), axis=0, keepdims=True).astype(I32)
    carry_ref[...] = carry + jnp.sum(onehot, axis=1, keepdims=True)
    meta_ref[...] = jnp.where(sub == 0, i1, jnp.where(sub == 1, i2, jnp.where(sub == 2, pos1, jnp.where(sub == 3, pos2, 0))))
    wts_ref[...] = jnp.where(sub == 0, w1, jnp.where(sub == 1, w2, 0.0))


def _router(h, gain, router_t):
    t, d = h.shape
    tm = min(MOE_BLK, t)
    nwin = t // tm
    r = jnp.arange(tm)[:, None]
    c = jnp.arange(tm)[None, :]
    us = (r < c).astype(BF16)
    return pl.pallas_call(
        _router_kernel,
        grid=(nwin,),
        in_specs=[
            pl.BlockSpec((tm, d), lambda i: (i, 0)),
            pl.BlockSpec((1, d), lambda i: (0, 0)),
            pl.BlockSpec((N_EXPERTS, d), lambda i: (0, 0)),
            pl.BlockSpec((tm, tm), lambda i: (0, 0)),
        ],
        out_specs=[
            pl.BlockSpec((tm, d), lambda i: (i, 0)),
            pl.BlockSpec((N_EXPERTS, tm), lambda i: (0, i)),
            pl.BlockSpec((N_EXPERTS, tm), lambda i: (0, i)),
            pl.BlockSpec((1, N_EXPERTS, LANES), lambda i: (i, 0, 0)),
        ],
        out_shape=[
            jax.ShapeDtypeStruct((t, d), BF16),
            jax.ShapeDtypeStruct((N_EXPERTS, t), I32),
            jax.ShapeDtypeStruct((N_EXPERTS, t), F32),
            jax.ShapeDtypeStruct((nwin, N_EXPERTS, LANES), F32),
        ],
        scratch_shapes=[pltpu.VMEM((N_EXPERTS, LANES), F32)],
        compiler_params=_cparams("arbitrary"),
        name="moe_router",
    )(h, gain.reshape(1, d), router_t, us)


def _dispatch_kernel(blk_ref, win_ref, flag_ref, hn_ref, slots_ref, wts_ref, xs_ref, ws_ref):
    l = pl.program_id(0)
    flag = flag_ref[l]
    nslot = xs_ref.shape[0]

    @pl.when((flag & 2) != 0)
    def _():
        xs_ref[...] = jnp.zeros_like(xs_ref)
        ws_ref[...] = jnp.zeros_like(ws_ref)

    @pl.when((flag & 1) != 0)
    def _():
        ntok = hn_ref.shape[0]
        slot = blk_ref[l] * nslot + lax.broadcasted_iota(I32, (nslot, ntok), 0)
        hit1 = slots_ref[0:1, :] == slot
        hit2 = slots_ref[1:2, :] == slot
        sel = jnp.where(hit1 | hit2, 1.0, 0.0).astype(BF16)
        xs_ref[...] += _dot(sel, hn_ref[...]).astype(xs_ref.dtype)
        wsel = jnp.where(hit1, wts_ref[0:1, :], 0.0) + jnp.where(hit2, wts_ref[1:2, :], 0.0)
        ws_ref[...] += jnp.sum(wsel, axis=1, keepdims=True)


def _dispatch(hn, slots_rows, wts, blk, win, flag, cap):
    t, d = hn.shape
    b = MOE_BLK
    n_pairs = blk.shape[0]
    return pl.pallas_call(
        _dispatch_kernel,
        grid_spec=pltpu.PrefetchScalarGridSpec(
            num_scalar_prefetch=3,
            grid=(n_pairs,),
            in_specs=[
                pl.BlockSpec((b, d), lambda l, blk, win, flag: (win[l], 0)),
                pl.BlockSpec((N_EXPERTS, b), lambda l, blk, win, flag: (0, win[l])),
                pl.BlockSpec((N_EXPERTS, b), lambda l, blk, win, flag: (0, win[l])),
            ],
            out_specs=[
                pl.BlockSpec((b, d), lambda l, blk, win, flag: (blk[l], 0)),
                pl.BlockSpec((b, LANES), lambda l, blk, win, flag: (blk[l], 0)),
            ],
        ),
        out_shape=[jax.ShapeDtypeStruct((cap, d), BF16), jax.ShapeDtypeStruct((cap, LANES), F32)],
        compiler_params=_cparams("arbitrary"),
        name="moe_dispatch",
    )(blk, win, flag, hn, slots_rows, wts)


def _experts_kernel(be_ref, src_ref, used_ref, xs_ref, ws_ref, wg_ref, wu_ref, wd_ref, y_ref, acc_ref):
    b = pl.program_id(0)
    f = pl.program_id(1)
    used = used_ref[b] != 0

    @pl.when(f == 0)
    def _():
        acc_ref[...] = jnp.zeros_like(acc_ref)

    @pl.when(used)
    def _():
        xs = xs_ref[...]
        g = _dot(xs, wg_ref[...])
        up = _dot(xs, wu_ref[...])
        act = (g * _sigmoid(g) * up).astype(BF16)
        acc_ref[...] += _dot(act, wd_ref[...])

    @pl.when(f == pl.num_programs(1) - 1)
    def _():
        y_ref[...] = jnp.where(used, acc_ref[...] * ws_ref[:, 0:1], 0.0).astype(y_ref.dtype)


def _experts(xs, ws, wg, wu, wd, blk_expert, blk_src, blk_used):
    cap, d = xs.shape
    ff = wg.shape[2]
    b = MOE_BLK
    tf = TF_MOE
    return pl.pallas_call(
        _experts_kernel,
        grid_spec=pltpu.PrefetchScalarGridSpec(
            num_scalar_prefetch=3,
            grid=(cap // b, ff // tf),
            in_specs=[
                pl.BlockSpec((b, d), lambda i, f, be, src, used: (src[i], 0)),
                pl.BlockSpec((b, LANES), lambda i, f, be, src, used: (src[i], 0)),
                pl.BlockSpec((None, d, tf), lambda i, f, be, src, used: (be[i], 0, f)),
                pl.BlockSpec((None, d, tf), lambda i, f, be, src, used: (be[i], 0, f)),
                pl.BlockSpec((None, tf, d), lambda i, f, be, src, used: (be[i], f, 0)),
            ],
            out_specs=pl.BlockSpec((b, d), lambda i, f, be, src, used: (i, 0)),
            scratch_shapes=[pltpu.VMEM((b, d), F32)],
        ),
        out_shape=jax.ShapeDtypeStruct((cap, d), BF16),
        compiler_params=_cparams("parallel", "arbitrary"),
        name="moe_experts",
    )(blk_expert, blk_src, blk_used, xs, ws, wg, wu, wd)


def _combine_kernel(win_ref, blk_ref, flag_ref, h_ref, y_ref, slots_ref, o_ref):
    l = pl.program_id(0)
    flag = flag_ref[l]

    @pl.when((flag & 2) != 0)
    def _():
        o_ref[...] = h_ref[...]

    @pl.when((flag & 1) != 0)
    def _():
        ntok = h_ref.shape[0]
        nslot = y_ref.shape[0]
        slot = blk_ref[l] * nslot + lax.broadcasted_iota(I32, (ntok, nslot), 1)
        hit = (slots_ref[:, 0:1] == slot) | (slots_ref[:, 1:2] == slot)
        o_ref[...] += _dot(jnp.where(hit, 1.0, 0.0).astype(BF16), y_ref[...])


def _combine(h, y, slots_cols, win, blk, flag):
    t, d = h.shape
    b = MOE_BLK
    n_pairs = win.shape[0]
    return pl.pallas_call(
        _combine_kernel,
        grid_spec=pltpu.PrefetchScalarGridSpec(
            num_scalar_prefetch=3,
            grid=(n_pairs,),
            in_specs=[
                pl.BlockSpec((b, d), lambda l, win, blk, flag: (win[l], 0)),
                pl.BlockSpec((b, d), lambda l, win, blk, flag: (blk[l], 0)),
                pl.BlockSpec((b, N_EXPERTS), lambda l, win, blk, flag: (win[l], 0)),
            ],
            out_specs=pl.BlockSpec((b, d), lambda l, win, blk, flag: (win[l], 0)),
        ),
        out_shape=jax.ShapeDtypeStruct((t, d), F32),
        compiler_params=_cparams("arbitrary"),
        name="moe_combine",
    )(win, blk, flag, h, y, slots_cols)


def _pair_list(first_blk, n_blk, order_major_is_expert, n_pairs):
    n_e, n_w = first_blk.shape
    if order_major_is_expert:
        fb, nb = first_blk.reshape(-1), n_blk.reshape(-1)
        win_of = jnp.tile(jnp.arange(n_w, dtype=I32), n_e)
    else:
        fb, nb = first_blk.T.reshape(-1), n_blk.T.reshape(-1)
        win_of = jnp.repeat(jnp.arange(n_w, dtype=I32), n_e)
    end = jnp.cumsum(nb)
    total = end[-1]
    l = jnp.arange(n_pairs, dtype=I32)
    lc = jnp.minimum(l, total - 1)
    p = jnp.searchsorted(end, lc, side="right").astype(I32)
    blk = fb[p] + (lc - (end[p] - nb[p]))
    win = win_of[p]
    valid = l < total
    return blk.astype(I32), win.astype(I32), valid


def _moe(h, gain, router, wg, wu, wd):
    t, d = h.shape
    b = MOE_BLK
    nwin = t // b
    nblk = (2 * t) // b + N_EXPERTS
    cap = nblk * b
    n_pairs = nblk + N_EXPERTS * nwin

    hn, meta, wts, cstart = _router(h, gain, router.T.astype(F32))
    e1, e2, pos1, pos2 = meta[0], meta[1], meta[2], meta[3]

    cstart = cstart[:, :, 0].astype(I32).T
    eid = jnp.arange(N_EXPERTS, dtype=I32)[:, None]
    hit1 = e1[None, :] == eid
    hit2 = e2[None, :] == eid
    counts = jnp.sum((hit1 | hit2).astype(I32), axis=1)
    padded = (counts + b - 1) // b * b
    gend = jnp.cumsum(padded)
    gstart = gend - padded
    slot1 = jnp.sum(jnp.where(hit1, gstart[:, None], 0), axis=0) + pos1
    slot2 = jnp.sum(jnp.where(hit2, gstart[:, None], 0), axis=0) + pos2
    slots_rows = jnp.zeros((N_EXPERTS, t), I32).at[0].set(slot1).at[1].set(slot2)
    slots_cols = slots_rows.T

    cend = jnp.concatenate([cstart[:, 1:], counts[:, None]], axis=1)
    lo = gstart[:, None] + cstart
    hi = gstart[:, None] + cend
    first_blk = lo // b
    n_blk = jnp.where(cend > cstart, (hi - 1) // b - first_blk + 1, 0)

    blk_d, win_d, valid_d = _pair_list(first_blk, n_blk, True, n_pairs)
    new_d = jnp.concatenate([jnp.ones((1,), bool), blk_d[1:] != blk_d[:-1]])
    flag_d = valid_d.astype(I32) + 2 * (new_d & valid_d).astype(I32)
    blk_c, win_c, valid_c = _pair_list(first_blk, n_blk, False, n_pairs)
    new_c = jnp.concatenate([jnp.ones((1,), bool), win_c[1:] != win_c[:-1]])
    flag_c = valid_c.astype(I32) + 2 * (new_c & valid_c).astype(I32)

    n_used = gend[-1] // b
    bidx = jnp.arange(nblk, dtype=I32)
    blk_used = (bidx < n_used).astype(I32)
    blk_src = jnp.minimum(bidx, n_used - 1)
    blk_expert = jnp.minimum(jnp.searchsorted(gend, blk_src * b, side="right"), N_EXPERTS - 1).astype(I32)

    xs, ws = _dispatch(hn, slots_rows, wts, blk_d, win_d, flag_d, cap)
    y = _experts(xs, ws, wg, wu, wd, blk_expert, blk_src, blk_used)
    return _combine(h, y, slots_cols, win_c, blk_c, flag_c)


def _tile_gain(gain, reps, scale=1.0):
    return (jnp.tile(gain.astype(F32), reps) * scale).reshape(1, -1)


def _mem_kv(mem2d, gain, w, k_gain):
    return _norm_proj(mem2d, gain, w.astype(BF16), _tile_gain(k_gain, XA_WIDTH // HEAD64), XA_WIDTH, 2 * XA_WIDTH)[0]


def kernel(x, mem, mix_norm, ffn_norm, mem_norm, w_mem_kv, xa_q_norm, xa_k_norm, w_out, sb_w_in, sb_q_norm, sb_k_norm, dn_w_in, dn_conv, dn_a_log, dn_dt_bias, dn_o_norm, ffd_w_gate, ffd_w_up, ffd_w_down, moe_router, moe_w_gate, moe_w_up, moe_w_down):
    batch, seq, d = x.shape
    mem_len = mem.shape[1]
    t = batch * seq
    h = x.reshape(t, d)
    mem2d = mem.reshape(batch * mem_len, d)
    scale64 = HEAD64 ** -0.5

    qk_gain = jnp.concatenate([_tile_gain(sb_q_norm[0], SB_HEADS, scale64 * LOG2E), _tile_gain(sb_k_norm[0], SB_HEADS)], axis=1)
    (proj0,) = _norm_proj(h, mix_norm[0], sb_w_in[0].astype(BF16), qk_gain, 2 * SB_WIDTH, 3 * SB_WIDTH + XA_WIDTH)
    mix0 = _sb_attention(proj0, batch, seq)
    memkv0 = _mem_kv(mem2d, mem_norm[0], w_mem_kv[0], xa_k_norm[0])
    xq_gain0 = _tile_gain(xa_q_norm[0], XA_WIDTH // HEAD64, scale64)
    h = _xattn_out(h, mix0, proj0, (3 * SB_WIDTH) // XA_WIDTH, memkv0, xq_gain0, w_out[0].astype(BF16), seq, mem_len)
    h = _ffn(h, ffn_norm[0], ffd_w_gate[0].astype(BF16), ffd_w_up[0].astype(BF16), ffd_w_down[0].astype(BF16))

    w1 = dn_w_in[0]
    n_main = 4 * DN_WIDTH + XA_WIDTH
    w1 = jnp.concatenate([w1[:, :4 * DN_WIDTH], w1[:, 4 * DN_WIDTH + 2 * DN_HEADS:], w1[:, 4 * DN_WIDTH:4 * DN_WIDTH + 2 * DN_HEADS],
                          jnp.zeros((d, LANES - 2 * DN_HEADS), F32)], axis=1).astype(BF16)
    proj1, ab = _norm_proj(h, mix_norm[1], w1, None, 0, n_main)
    pad = LANES - DN_HEADS
    alog_row = jnp.pad(dn_a_log[0].astype(F32), (0, pad)).reshape(1, LANES)
    dtb_row = jnp.pad(dn_dt_bias[0].astype(F32), (0, pad)).reshape(1, LANES)
    qkvn, gcb, gl = _dn_prep(proj1, ab, dn_conv[0].astype(F32), alog_row, dtb_row, seq)
    gct = gcb[:, :DN_HEADS].T.reshape(DN_HEADS, 1, t)
    mix1 = _delta_rule(qkvn, proj1, gcb, gl, gct, dn_o_norm[0].astype(F32).reshape(1, DN_HEAD), batch, seq)
    memkv1 = _mem_kv(mem2d, mem_norm[1], w_mem_kv[1], xa_k_norm[1])
    xq_gain1 = _tile_gain(xa_q_norm[1], XA_WIDTH // HEAD64, scale64)
    h = _xattn_out(h, mix1, proj1, (4 * DN_WIDTH) // XA_WIDTH, memkv1, xq_gain1, w_out[1].astype(BF16), seq, mem_len)
    h = _moe(h, ffn_norm[1], moe_router[0], moe_w_gate[0].astype(BF16), moe_w_up[0].astype(BF16), moe_w_down[0].astype(BF16))
    return h.reshape(batch, seq, d)
```

```python
import functools

import jax
import jax.numpy as jnp
from jax import lax
from jax.experimental import pallas as pl
from jax.experimental.pallas import tpu as pltpu

F32 = jnp.float32
BF16 = jnp.bfloat16
I32 = jnp.int32

EPS = 1e-6
LOG2E = 1.4426950408889634
D_MODEL = 1024
HEAD64 = 64
SB_HEADS = 12
SB_WIDTH = SB_HEADS * HEAD64
XA_WIDTH = 256
DN_HEADS = 6
DN_HEAD = 128
DN_WIDTH = DN_HEADS * DN_HEAD
DN_CONV = 4
DN_CHUNK = 64
D_FF = 3584
N_EXPERTS = 8

LANES = 128
MXU_DIM = 256
VMEM_LIMIT = 56 * 1024 * 1024

TM_PROJ = 512
TM_FFN = 1024
TF_FFN = 512
TF_MOE = 1792
SB_TQ = 512
SB_TK = 256
SB_ZERO_LOG2 = -160.0
DN_ROWS = 256
DN_HEADS_PER_STEP = 6
MOE_BLK = 512


def _cparams(*sem):
    return pltpu.CompilerParams(dimension_semantics=sem, vmem_limit_bytes=VMEM_LIMIT)


def _dot(a, b):
    return jnp.dot(a, b, preferred_element_type=F32)


def _dot_nt(a, b):
    return lax.dot_general(a, b, (((1,), (1,)), ((), ())), preferred_element_type=F32)


def _dot_tn(a, b):
    return lax.dot_general(a, b, (((0,), (0,)), ((), ())), preferred_element_type=F32)


def _rms_rows(x, gain_row):
    ms = jnp.mean(x * x, axis=-1, keepdims=True)
    return x * lax.rsqrt(ms + EPS) * gain_row


def _softplus(x):
    return jnp.maximum(x, 0.0) + jnp.log1p(jnp.exp(-jnp.abs(x)))


def _sigmoid(x):
    return 1.0 / (1.0 + jnp.exp(-x))


def _head_sum_matrix(width, head):
    r = jnp.arange(width)[:, None] // head
    c = jnp.arange(width)[None, :] // head
    return (r == c).astype(BF16)


def _norm_proj_kernel(x_ref, g_ref, w_ref, hg_ref, hm_ref, *o_refs, n_norm_cols, n_main_cols):
    xn = _rms_rows(x_ref[...], g_ref[...]).astype(BF16)
    proj = _dot(xn, w_ref[...])
    o_ref = o_refs[0]
    for c in range(n_norm_cols // MXU_DIM):
        cols = slice(c * MXU_DIM, (c + 1) * MXU_DIM)
        blk = proj[:, cols]
        ss = _dot((blk * blk).astype(BF16), hm_ref[...])
        o_ref[:, cols] = (blk * lax.rsqrt(ss * (1.0 / HEAD64) + EPS) * hg_ref[:, cols]).astype(o_ref.dtype)
    if n_main_cols > n_norm_cols:
        o_ref[:, n_norm_cols:n_main_cols] = proj[:, n_norm_cols:n_main_cols].astype(o_ref.dtype)
    if len(o_refs) > 1:
        o_refs[1][...] = proj[:, n_main_cols:]


def _norm_proj(x, gain, w, head_gain, n_norm_cols, n_main_cols):
    t, d = x.shape
    n = w.shape[1]
    tm = min(TM_PROJ, t)
    out_shape = [jax.ShapeDtypeStruct((t, n_main_cols), BF16)]
    out_specs = [pl.BlockSpec((tm, n_main_cols), lambda i: (i, 0))]
    if n > n_main_cols:
        out_shape.append(jax.ShapeDtypeStruct((t, n - n_main_cols), F32))
        out_specs.append(pl.BlockSpec((tm, n - n_main_cols), lambda i: (i, 0)))
    hg = jnp.zeros((1, max(n_norm_cols, MXU_DIM)), F32) if head_gain is None else head_gain
    res = pl.pallas_call(
        functools.partial(_norm_proj_kernel, n_norm_cols=n_norm_cols, n_main_cols=n_main_cols),
        grid=(t // tm,),
        in_specs=[
            pl.BlockSpec((tm, d), lambda i: (i, 0)),
            pl.BlockSpec((1, d), lambda i: (0, 0)),
            pl.BlockSpec((d, n), lambda i: (0, 0)),
            pl.BlockSpec(hg.shape, lambda i: (0, 0)),
            pl.BlockSpec((MXU_DIM, MXU_DIM), lambda i: (0, 0)),
        ],
        out_specs=out_specs,
        out_shape=out_shape,
        compiler_params=_cparams("parallel"),
        name="norm_proj",
    )(x, gain.reshape(1, d), w, hg, _head_sum_matrix(MXU_DIM, HEAD64))
    return res


def _sb_attn_kernel(qt_ref, k_ref, v_ref, nut_ref, o_ref):
    i = pl.program_id(2)
    tq = qt_ref.shape[1]
    tk = nut_ref.shape[0]
    qt = qt_ref[...]
    sub = lax.broadcasted_iota(I32, qt.shape, 0)
    zero = jnp.zeros_like(qt)
    qqt = jnp.concatenate([jnp.where(sub < HEAD64, qt, zero), jnp.where(sub >= HEAD64, qt, zero)], axis=1)
    nut = nut_ref[...]

    def tile_pair(p, carry, masked):
        offs = [pl.multiple_of((2 * p + 1) * tk, tk), pl.multiple_of(2 * p * tk, tk)]
        zs = [_dot(k_ref[pl.ds(off, tk), :], qqt) for off in offs]
        log_b, spb, causal, within, pv = [], [], [], [], []
        for n, off in enumerate(offs):
            z = zs[n]
            sp = jnp.maximum(z, jnp.log2(1.0 + jnp.exp2(jnp.minimum(z, 126.0))))
            log_b.append(z - sp)
            if masked:
                key = off + lax.broadcasted_iota(I32, z.shape, 0)
                qi = lax.broadcasted_iota(I32, z.shape, 1)
                causal.append(key < i * tq + jnp.where(qi >= tq, qi - tq, qi))
                sp = jnp.where(causal[n], sp, 0.0)
            spb.append(sp.astype(BF16))
            within.append(_dot(nut, spb[n]))
        for n, off in enumerate(offs):
            arg = log_b[n] + within[n] + carry
            if masked:
                arg = jnp.where(causal[n], arg, -1e30)
            pv.append(_dot_tn(v_ref[pl.ds(off, tk), :], jnp.exp2(arg).astype(BF16)))
            carry = carry + within[n][0:1, :] - spb[n][0:1, :].astype(F32)
        return pv[0] + pv[1], carry

    last = ((i + 1) * tq - 1) // (2 * tk)
    init = tile_pair(last, jnp.zeros((1, 2 * tq), F32), True)

    def live(st):
        return (st[0] < last) & (jnp.max(st[2]) > SB_ZERO_LOG2)

    def body(st):
        jj, acc, carry = st
        pv, carry = tile_pair(last - 1 - jj, carry, False)
        return jj + 1, acc + pv, carry

    _, acc, _ = lax.while_loop(live, body, (jnp.int32(0),) + init)
    lane = lax.broadcasted_iota(I32, (tq, LANES), 1)
    o_ref[...] = jnp.where(lane < HEAD64, acc[:, :tq].T, acc[:, tq:].T).astype(o_ref.dtype)


def _sb_attention(proj, batch, seq):
    tq = min(SB_TQ, seq)
    nq = seq // tq
    n_pairs = SB_WIDTH // LANES
    qt = proj[:, :SB_WIDTH].T
    tk = min(SB_TK, seq // 2)
    s_row = jnp.arange(tk)[:, None]
    j_col = jnp.arange(tk)[None, :]
    nut = -(j_col > s_row).astype(BF16)
    return pl.pallas_call(
        _sb_attn_kernel,
        grid=(batch, n_pairs, nq),
        in_specs=[
            pl.BlockSpec((LANES, tq), lambda b, p, i: (p, b * nq + i)),
            pl.BlockSpec((seq, LANES), lambda b, p, i: (b, n_pairs + p)),
            pl.BlockSpec((seq, LANES), lambda b, p, i: (b, 2 * n_pairs + p)),
            pl.BlockSpec((tk, tk), lambda b, p, i: (0, 0)),
        ],
        out_specs=pl.BlockSpec((tq, LANES), lambda b, p, i: (b * nq + i, p)),
        out_shape=jax.ShapeDtypeStruct((batch * seq, SB_WIDTH), BF16),
        compiler_params=_cparams("parallel", "parallel", "arbitrary"),
        name="sb_attention",
    )(qt, proj, proj, nut)


def _xattn_out_kernel(h_ref, mix_ref, xq_ref, mk_ref, mv_ref, qg_ref, hm_ref, wo_ref, o_ref):
    tm = h_ref.shape[0]
    xq = xq_ref[...].astype(F32)
    ss = _dot((xq * xq).astype(BF16), hm_ref[...])
    qn = (xq * lax.rsqrt(ss * (1.0 / HEAD64) + EPS) * qg_ref[...]).astype(BF16)
    lane = lax.broadcasted_iota(I32, (tm, XA_WIDTH), 1)
    mk = mk_ref[...]
    mv = mv_ref[...]
    xa = jnp.zeros((tm, XA_WIDTH), F32)
    for hh in range(XA_WIDTH // HEAD64):
        in_head = (lane >= hh * HEAD64) & (lane < (hh + 1) * HEAD64)
        s = _dot_nt(jnp.where(in_head, qn, jnp.zeros_like(qn)), mk)
        p = jnp.exp(s - jnp.max(s, axis=1, keepdims=True))
        o = _dot(p.astype(BF16), mv) / jnp.sum(p, axis=1, keepdims=True)
        xa = jnp.where(in_head, o, xa)
    mw = mix_ref.shape[1]
    acc = _dot(mix_ref[...], wo_ref[0:mw, :]) + _dot(xa.astype(BF16), wo_ref[mw:, :])
    o_ref[...] = h_ref[...] + acc


def _xattn_out(h, mix, proj, xq_col_block, memkv, q_gain_row, w_out, seq, mem_len):
    t, d = h.shape
    tm = min(TM_PROJ, seq)
    per_b = seq // tm
    mw = mix.shape[1]
    return pl.pallas_call(
        _xattn_out_kernel,
        grid=(t // tm,),
        in_specs=[
            pl.BlockSpec((tm, d), lambda i: (i, 0)),
            pl.BlockSpec((tm, mw), lambda i: (i, 0)),
            pl.BlockSpec((tm, XA_WIDTH), lambda i: (i, xq_col_block)),
            pl.BlockSpec((mem_len, XA_WIDTH), lambda i: (i // per_b, 0)),
            pl.BlockSpec((mem_len, XA_WIDTH), lambda i: (i // per_b, 1)),
            pl.BlockSpec((1, XA_WIDTH), lambda i: (0, 0)),
            pl.BlockSpec((XA_WIDTH, XA_WIDTH), lambda i: (0, 0)),
            pl.BlockSpec((mw + XA_WIDTH, d), lambda i: (0, 0)),
        ],
        out_specs=pl.BlockSpec((tm, d), lambda i: (i, 0)),
        out_shape=jax.ShapeDtypeStruct((t, d), F32),
        compiler_params=_cparams("parallel"),
        name="xattn_out",
    )(h, mix, proj, memkv, memkv, q_gain_row, _head_sum_matrix(XA_WIDTH, HEAD64), w_out)


def _ffn_kernel(x_ref, g_ref, wg_ref, wu_ref, wd_ref, o_ref, xn_ref, acc_ref):
    f = pl.program_id(1)

    @pl.when(f == 0)
    def _():
        xn_ref[...] = _rms_rows(x_ref[...], g_ref[...]).astype(BF16)
        acc_ref[...] = jnp.zeros_like(acc_ref)

    xn = xn_ref[...]
    g = _dot(xn, wg_ref[...])
    up = _dot(xn, wu_ref[...])
    act = (g * _sigmoid(g) * up).astype(BF16)
    acc_ref[...] += _dot(act, wd_ref[...])

    @pl.when(f == pl.num_programs(1) - 1)
    def _():
        o_ref[...] = x_ref[...] + acc_ref[...]


def _ffn(h, gain, wg, wu, wd):
    t, d = h.shape
    ff = wg.shape[1]
    tm = min(TM_FFN, t)
    tf = TF_FFN
    return pl.pallas_call(
        _ffn_kernel,
        grid=(t // tm, ff // tf),
        in_specs=[
            pl.BlockSpec((tm, d), lambda i, f: (i, 0)),
            pl.BlockSpec((1, d), lambda i, f: (0, 0)),
            pl.BlockSpec((d, tf), lambda i, f: (0, f)),
            pl.BlockSpec((d, tf), lambda i, f: (0, f)),
            pl.BlockSpec((tf, d), lambda i, f: (f, 0)),
        ],
        out_specs=pl.BlockSpec((tm, d), lambda i, f: (i, 0)),
        out_shape=jax.ShapeDtypeStruct((t, d), F32),
        scratch_shapes=[pltpu.VMEM((tm, d), BF16), pltpu.VMEM((tm, d), F32)],
        compiler_params=_cparams("parallel", "arbitrary"),
        name="ffn_dense",
    )(h, gain.reshape(1, d), wg, wu, wd)


def _dn_prep_kernel(x_ref, halo_ref, ab_ref, cw_ref, alog_ref, dtb_ref, tril_ref, ones_ref,
                    qkv_ref, gcb_ref, gl_ref, *, per_b):
    i = pl.program_id(0)
    tm = x_ref.shape[0]
    x = x_ref[...].astype(F32)
    keep = (i % per_b != 0).astype(F32)
    halo = halo_ref[...].astype(F32)[-8:, :] * keep
    row8 = lax.broadcasted_iota(I32, (8, x.shape[1]), 0)
    cw = cw_ref[...]
    acc = x * cw[DN_CONV - 1:DN_CONV, :]
    for s in range(1, DN_CONV):
        xs = pltpu.roll(x, s, 0)
        hs = pltpu.roll(halo, s, 0)
        first = jnp.where(row8 < s, hs, xs[0:8, :])
        xs = jnp.concatenate([first, xs[8:, :]], axis=0)
        acc = acc + xs * cw[DN_CONV - 1 - s:DN_CONV - s, :]
    y = acc * _sigmoid(acc)
    for c in range(3 * DN_HEADS):
        cols = slice(c * DN_HEAD, (c + 1) * DN_HEAD)
        blk = y[:, cols]
        if c < 2 * DN_HEADS:
            blk = blk * lax.rsqrt(jnp.sum(blk * blk, axis=-1, keepdims=True) + EPS)
            if c < DN_HEADS:
                blk = blk * (DN_HEAD ** -0.5)
        qkv_ref[:, cols] = blk.astype(qkv_ref.dtype)

    ab = ab_ref[...]
    lane = lax.broadcasted_iota(I32, ab.shape, 1)
    g = -jnp.exp(alog_ref[...]) * _softplus(ab + dtb_ref[...])
    g = jnp.where(lane < DN_HEADS, g, 0.0)
    beta = _sigmoid(ab)
    g_hi = g.astype(BF16)
    g_mid = (g - g_hi.astype(F32)).astype(BF16)
    g_lo = (g - g_hi.astype(F32) - g_mid.astype(F32)).astype(BF16)
    tril = tril_ref[...]
    ones = ones_ref[...]
    gc = _dot(tril, g_hi) + _dot(tril, g_mid) + _dot(tril, g_lo)
    gl = _dot(ones, g_hi) + _dot(ones, g_mid) + _dot(ones, g_lo)
    gcb_ref[...] = jnp.where(lane < DN_HEADS, gc, beta)
    gl_ref[...] = gl


def _dn_prep(proj, ab, conv_w, alog_row, dtb_row, seq):
    t = proj.shape[0]
    tm = min(TM_PROJ, seq)
    per_b = seq // tm
    w3 = 3 * DN_WIDTH
    hb = 16
    r = jnp.arange(tm)[:, None]
    c = jnp.arange(tm)[None, :]
    same = (r // DN_CHUNK) == (c // DN_CHUNK)
    tril = (same & (c <= r)).astype(BF16)
    ones = same.astype(BF16)
    return pl.pallas_call(
        functools.partial(_dn_prep_kernel, per_b=per_b),
        grid=(t // tm,),
        in_specs=[
            pl.BlockSpec((tm, w3), lambda i: (i, 0)),
            pl.BlockSpec((hb, w3), lambda i: (jnp.maximum(i * (tm // hb) - 1, 0), 0)),
            pl.BlockSpec((tm, LANES), lambda i: (i, 0)),
            pl.BlockSpec((DN_CONV, w3), lambda i: (0, 0)),
            pl.BlockSpec((1, LANES), lambda i: (0, 0)),
            pl.BlockSpec((1, LANES), lambda i: (0, 0)),
            pl.BlockSpec((tm, tm), lambda i: (0, 0)),
            pl.BlockSpec((tm, tm), lambda i: (0, 0)),
        ],
        out_specs=[
            pl.BlockSpec((tm, w3), lambda i: (i, 0)),
            pl.BlockSpec((tm, LANES), lambda i: (i, 0)),
            pl.BlockSpec((tm, LANES), lambda i: (i, 0)),
        ],
        out_shape=[
            jax.ShapeDtypeStruct((t, w3), BF16),
            jax.ShapeDtypeStruct((t, LANES), F32),
            jax.ShapeDtypeStruct((t, LANES), F32),
        ],
        compiler_params=_cparams("parallel"),
        name="dn_prep",
    )(proj, proj, ab, conv_w, alog_row, dtb_row, tril, ones)


def _delta_kernel(q_ref, k_ref, v_ref, gate_ref, gcb_ref, gl_ref, gct_ref, og_ref, o_ref, s_ref, *, heads):
    h0 = pl.program_id(1) * heads
    rows = q_ref.shape[0]
    n_chunks = rows // DN_CHUNK
    hs = range(heads)

    @pl.when(pl.program_id(2) == 0)
    def _():
        s_ref[...] = jnp.zeros_like(s_ref)

    lane = lax.broadcasted_iota(I32, (rows, LANES), 1)
    gcb = gcb_ref[...]
    glb = gl_ref[...]
    gc_col = [jnp.sum(jnp.where(lane == h0 + h, gcb, 0.0), axis=1, keepdims=True) for h in hs]
    beta_col = [jnp.sum(jnp.where(lane == h0 + h + DN_HEADS, gcb, 0.0), axis=1, keepdims=True) for h in hs]
    gl_col = [jnp.sum(jnp.where(lane == h0 + h, glb, 0.0), axis=1, keepdims=True) for h in hs]

    ri = lax.broadcasted_iota(I32, (rows, rows), 0)
    ci = lax.broadcasted_iota(I32, (rows, rows), 1)
    same = (ri // DN_CHUNK) == (ci // DN_CHUNK)
    tri = same & (ci <= ri)
    strict = same & (ci < ri)
    eye = (ri == ci).astype(F32)
    cols = [slice(h * DN_HEAD, (h + 1) * DN_HEAD) for h in hs]

    decay = [jnp.exp(jnp.where(tri, gc_col[h] - gct_ref[h], -1e30)) for h in hs]
    k = [k_ref[:, cols[h]] for h in hs]
    q = [q_ref[:, cols[h]] for h in hs]
    kf = [k[h].astype(F32) for h in hs]
    eg = [jnp.exp(gc_col[h]) for h in hs]
    kb = [kf[h] * beta_col[h] for h in hs]
    rhs = [jnp.concatenate([v_ref[:, cols[h]].astype(F32) * beta_col[h], kb[h] * eg[h]], axis=1).astype(BF16) for h in hs]
    npow = [jnp.where(strict, -(_dot_nt(kb[h].astype(BF16), k[h]) * decay[h]), 0.0) for h in hs]
    inv = [eye + npow[h] for h in hs]
    for _ in range(5):
        nb = [npow[h].astype(BF16) for h in hs]
        npow = [_dot(nb[h], nb[h]) for h in hs]
        inv = [inv[h] + _dot(npow[h].astype(BF16), inv[h].astype(BF16)) for h in hs]
    uw = [_dot(inv[h].astype(BF16), rhs[h]) for h in hs]
    qk = [(_dot_nt(q[h], k[h]) * decay[h]).astype(BF16) for h in hs]
    w_b = [uw[h][:, DN_HEAD:].astype(BF16) for h in hs]
    q_dec = [(q[h].astype(F32) * eg[h]).astype(BF16) for h in hs]
    k_dec = [(kf[h] * jnp.exp(gl_col[h] - gc_col[h])).astype(BF16) for h in hs]

    state = [s_ref[h] for h in hs]
    outs = [[] for _ in hs]
    for c in range(n_chunks):
        rs = slice(c * DN_CHUNK, (c + 1) * DN_CHUNK)
        sb = [state[h].astype(BF16) for h in hs]
        ws = [_dot(jnp.concatenate([w_b[h][rs], q_dec[h][rs]], axis=0), sb[h]) for h in hs]
        v_new = [(uw[h][rs, :DN_HEAD] - ws[h][:DN_CHUNK]).astype(BF16) for h in hs]
        for h in hs:
            outs[h].append(ws[h][DN_CHUNK:] + _dot(qk[h][rs, rs], v_new[h]))
        state = [state[h] * jnp.exp(gl_col[h][c * DN_CHUNK:c * DN_CHUNK + 1, :]) + _dot_tn(k_dec[h][rs], v_new[h]) for h in hs]
    for h in hs:
        s_ref[h] = state[h]
        o = jnp.concatenate(outs[h], axis=0)
        gate = gate_ref[:, cols[h]].astype(F32)
        o_ref[:, cols[h]] = (_rms_rows(o, og_ref[...]) * (gate * _sigmoid(gate))).astype(o_ref.dtype)


def _delta_rule(qkvn, proj, gcb, gl, gct, o_gain_row, batch, seq):
    rows = min(DN_ROWS, seq)
    steps = seq // rows
    t = batch * seq
    heads = DN_HEADS_PER_STEP
    groups = DN_HEADS // heads
    w = heads * DN_HEAD
    return pl.pallas_call(
        functools.partial(_delta_kernel, heads=heads),
        grid=(batch, groups, steps),
        in_specs=[
            pl.BlockSpec((rows, w), lambda b, g, s: (b * steps + s, g)),
            pl.BlockSpec((rows, w), lambda b, g, s: (b * steps + s, groups + g)),
            pl.BlockSpec((rows, w), lambda b, g, s: (b * steps + s, 2 * groups + g)),
            pl.BlockSpec((rows, w), lambda b, g, s: (b * steps + s, 3 * groups + g)),
            pl.BlockSpec((rows, LANES), lambda b, g, s: (b * steps + s, 0)),
            pl.BlockSpec((rows, LANES), lambda b, g, s: (b * steps + s, 0)),
            pl.BlockSpec((heads, 1, rows), lambda b, g, s: (g, 0, b * steps + s)),
            pl.BlockSpec((1, DN_HEAD), lambda b, g, s: (0, 0)),
        ],
        out_specs=pl.BlockSpec((rows, w), lambda b, g, s: (b * steps + s, g)),
        out_shape=jax.ShapeDtypeStruct((t, DN_WIDTH), BF16),
        scratch_shapes=[pltpu.VMEM((heads, DN_HEAD, DN_HEAD), F32)],
        compiler_params=_cparams("parallel", "parallel", "arbitrary"),
        name="delta_rule",
    )(qkvn, qkvn, qkvn, proj, gcb, gl, gct, o_gain_row)


def _router_kernel(h_ref, g_ref, rt_ref, us_ref, hn_ref, pos_ref, wts_ref, cstart_ref, carry_ref):
    i = pl.program_id(0)

    @pl.when(i == 0)
    def _():
        carry_ref[...] = jnp.zeros_like(carry_ref)

    hn = _rms_rows(h_ref[...], g_ref[...])
    hn_ref[...] = hn.astype(BF16)
    logits = lax.dot_general(rt_ref[...], hn, (((1,), (1,)), ((), ())),
                             precision=lax.Precision.HIGHEST, preferred_element_type=F32)
    sub = lax.broadcasted_iota(I32, logits.shape, 0)
    m1 = jnp.max(logits, axis=0, keepdims=True)
    i1 = jnp.min(jnp.where(logits == m1, sub, N_EXPERTS), axis=0, keepdims=True)
    rest = jnp.where(sub == i1, -jnp.inf, logits)
    m2 = jnp.max(rest, axis=0, keepdims=True)
    i2 = jnp.min(jnp.where(rest == m2, sub, N_EXPERTS), axis=0, keepdims=True)
    e21 = jnp.exp(m2 - m1)
    w1 = 1.0 / (1.0 + e21)
    w2 = e21 / (1.0 + e21)
    hit1 = sub == i1
    hit2 = sub == i2
    chosen = hit1 | hit2
    onehot = jnp.where(chosen, 1.0, 0.0)
    carry = carry_ref[...]
    cstart_ref[0] = carry
    before = _dot(onehot.astype(BF16), us_ref[...]) + carry[:, 0:1]
    carry_ref[...] = carry + jnp.sum(onehot, axis=1, keepdims=True)
    pos_ref[...] = jnp.where(chosen, before, -1.0)
    wts_ref[...] = jnp.where(hit1, w1, jnp.where(hit2, w2, 0.0))


def _router(h, gain, router_t):
    t, d = h.shape
    tm = min(MOE_BLK, t)
    nwin = t // tm
    r = jnp.arange(tm)[:, None]
    c = jnp.arange(tm)[None, :]
    us = (r < c).astype(BF16)
    return pl.pallas_call(
        _router_kernel,
        grid=(nwin,),
        in_specs=[
            pl.BlockSpec((tm, d), lambda i: (i, 0)),
            pl.BlockSpec((1, d), lambda i: (0, 0)),
            pl.BlockSpec((N_EXPERTS, d), lambda i: (0, 0)),
            pl.BlockSpec((tm, tm), lambda i: (0, 0)),
        ],
        out_specs=[
            pl.BlockSpec((tm, d), lambda i: (i, 0)),
            pl.BlockSpec((N_EXPERTS, tm), lambda i: (0, i)),
            pl.BlockSpec((N_EXPERTS, tm), lambda i: (0, i)),
            pl.BlockSpec((1, N_EXPERTS, LANES), lambda i: (i, 0, 0)),
        ],
        out_shape=[
            jax.ShapeDtypeStruct((t, d), BF16),
            jax.ShapeDtypeStruct((N_EXPERTS, t), F32),
            jax.ShapeDtypeStruct((N_EXPERTS, t), F32),
            jax.ShapeDtypeStruct((nwin, N_EXPERTS, LANES), F32),
        ],
        scratch_shapes=[pltpu.VMEM((N_EXPERTS, LANES), F32)],
        compiler_params=_cparams("arbitrary"),
        name="moe_router",
    )(h, gain.reshape(1, d), router_t, us)


def _dispatch_kernel(blk_ref, win_ref, flag_ref, exp_ref, hn_ref, slots_ref, wts_ref, xs_ref, ws_ref):
    l = pl.program_id(0)
    flag = flag_ref[l]
    nslot = xs_ref.shape[0]

    @pl.when((flag & 2) != 0)
    def _():
        xs_ref[...] = jnp.zeros_like(xs_ref)
        ws_ref[...] = jnp.zeros_like(ws_ref)

    @pl.when((flag & 1) != 0)
    def _():
        ntok = hn_ref.shape[0]
        e = exp_ref[l]
        base = (blk_ref[l] * nslot).astype(F32)
        rel = slots_ref[pl.ds(e, 1), :] - base
        hit = rel == lax.broadcasted_iota(I32, (nslot, ntok), 0).astype(F32)
        xs_ref[...] += _dot(jnp.where(hit, 1.0, 0.0).astype(BF16), hn_ref[...]).astype(xs_ref.dtype)
        ws_ref[...] += jnp.sum(jnp.where(hit, wts_ref[pl.ds(e, 1), :], 0.0), axis=1, keepdims=True)


def _dispatch(hn, slots_rows, wts, blk, win, flag, exp, cap):
    t, d = hn.shape
    b = MOE_BLK
    n_pairs = blk.shape[0]
    return pl.pallas_call(
        _dispatch_kernel,
        grid_spec=pltpu.PrefetchScalarGridSpec(
            num_scalar_prefetch=4,
            grid=(n_pairs,),
            in_specs=[
                pl.BlockSpec((b, d), lambda l, blk, win, flag, exp: (win[l], 0)),
                pl.BlockSpec((N_EXPERTS, b), lambda l, blk, win, flag, exp: (0, win[l])),
                pl.BlockSpec((N_EXPERTS, b), lambda l, blk, win, flag, exp: (0, win[l])),
            ],
            out_specs=[
                pl.BlockSpec((b, d), lambda l, blk, win, flag, exp: (blk[l], 0)),
                pl.BlockSpec((b, LANES), lambda l, blk, win, flag, exp: (blk[l], 0)),
            ],
        ),
        out_shape=[jax.ShapeDtypeStruct((cap, d), BF16), jax.ShapeDtypeStruct((cap, LANES), F32)],
        compiler_params=_cparams("arbitrary"),
        name="moe_dispatch",
    )(blk, win, flag, exp, hn, slots_rows, wts)


def _experts_kernel(be_ref, src_ref, used_ref, xs_ref, ws_ref, wg_ref, wu_ref, wd_ref, y_ref, acc_ref):
    b = pl.program_id(0)
    f = pl.program_id(1)
    used = used_ref[b] != 0

    @pl.when(f == 0)
    def _():
        acc_ref[...] = jnp.zeros_like(acc_ref)

    @pl.when(used)
    def _():
        xs = xs_ref[...]
        g = _dot(xs, wg_ref[...])
        up = _dot(xs, wu_ref[...])
        act = (g * _sigmoid(g) * up).astype(BF16)
        acc_ref[...] += _dot(act, wd_ref[...])

    @pl.when(f == pl.num_programs(1) - 1)
    def _():
        y_ref[...] = jnp.where(used, acc_ref[...] * ws_ref[:, 0:1], 0.0).astype(y_ref.dtype)


def _experts(xs, ws, wg, wu, wd, blk_expert, blk_src, blk_used):
    cap, d = xs.shape
    ff = wg.shape[2]
    b = MOE_BLK
    tf = TF_MOE
    return pl.pallas_call(
        _experts_kernel,
        grid_spec=pltpu.PrefetchScalarGridSpec(
            num_scalar_prefetch=3,
            grid=(cap // b, ff // tf),
            in_specs=[
                pl.BlockSpec((b, d), lambda i, f, be, src, used: (src[i], 0)),
                pl.BlockSpec((b, LANES), lambda i, f, be, src, used: (src[i], 0)),
                pl.BlockSpec((None, d, tf), lambda i, f, be, src, used: (be[i], 0, f)),
                pl.BlockSpec((None, d, tf), lambda i, f, be, src, used: (be[i], 0, f)),
                pl.BlockSpec((None, tf, d), lambda i, f, be, src, used: (be[i], f, 0)),
            ],
            out_specs=pl.BlockSpec((b, d), lambda i, f, be, src, used: (i, 0)),
            scratch_shapes=[pltpu.VMEM((b, d), F32)],
        ),
        out_shape=jax.ShapeDtypeStruct((cap, d), BF16),
        compiler_params=_cparams("parallel", "arbitrary"),
        name="moe_experts",
    )(blk_expert, blk_src, blk_used, xs, ws, wg, wu, wd)


def _combine_kernel(win_ref, blk_ref, flag_ref, exp_ref, h_ref, y_ref, slots_ref, o_ref):
    l = pl.program_id(0)
    flag = flag_ref[l]

    @pl.when((flag & 2) != 0)
    def _():
        o_ref[...] = h_ref[...]

    @pl.when((flag & 1) != 0)
    def _():
        ntok = h_ref.shape[0]
        nslot = y_ref.shape[0]
        cols = slots_ref[...]
        mine = lax.broadcasted_iota(I32, cols.shape, 1) == exp_ref[l]
        base = (blk_ref[l] * nslot).astype(F32)
        rel = jnp.sum(jnp.where(mine, cols, 0.0), axis=1, keepdims=True) - base
        hit = rel == lax.broadcasted_iota(I32, (ntok, nslot), 1).astype(F32)
        o_ref[...] += _dot(jnp.where(hit, 1.0, 0.0).astype(BF16), y_ref[...])


def _combine(h, y, slots_cols, win, blk, flag, exp):
    t, d = h.shape
    b = MOE_BLK
    n_pairs = win.shape[0]
    return pl.pallas_call(
        _combine_kernel,
        grid_spec=pltpu.PrefetchScalarGridSpec(
            num_scalar_prefetch=4,
            grid=(n_pairs,),
            in_specs=[
                pl.BlockSpec((b, d), lambda l, win, blk, flag, exp: (win[l], 0)),
                pl.BlockSpec((b, d), lambda l, win, blk, flag, exp: (blk[l], 0)),
                pl.BlockSpec((b, N_EXPERTS), lambda l, win, blk, flag, exp: (win[l], 0)),
            ],
            out_specs=pl.BlockSpec((b, d), lambda l, win, blk, flag, exp: (win[l], 0)),
        ),
        out_shape=jax.ShapeDtypeStruct((t, d), F32),
        compiler_params=_cparams("arbitrary"),
        name="moe_combine",
    )(win, blk, flag, exp, h, y, slots_cols)


def _pair_list(first_blk, n_blk, expert_major, n_pairs):
    n_e, n_w = first_blk.shape
    if expert_major:
        fb, nb = first_blk.reshape(-1), n_blk.reshape(-1)
        win_of = jnp.tile(jnp.arange(n_w, dtype=I32), n_e)
        exp_of = jnp.repeat(jnp.arange(n_e, dtype=I32), n_w)
    else:
        fb, nb = first_blk.T.reshape(-1), n_blk.T.reshape(-1)
        win_of = jnp.repeat(jnp.arange(n_w, dtype=I32), n_e)
        exp_of = jnp.tile(jnp.arange(n_e, dtype=I32), n_w)
    end = jnp.cumsum(nb)
    total = end[-1]
    l = jnp.arange(n_pairs, dtype=I32)
    lc = jnp.minimum(l, total - 1)
    p = jnp.sum((end[None, :] <= lc[:, None]).astype(I32), axis=1)
    onehot = jnp.arange(fb.shape[0], dtype=I32)[None, :] == p[:, None]

    def pick(v):
        return jnp.sum(jnp.where(onehot, v[None, :], 0), axis=1)

    blk = pick(fb) + (lc - (pick(end) - pick(nb)))
    win = pick(win_of)
    major = blk if expert_major else win
    first = jnp.concatenate([jnp.ones((1,), bool), major[1:] != major[:-1]])
    valid = l < total
    flag = valid.astype(I32) + 2 * (first & valid).astype(I32)
    return blk, win, pick(exp_of), flag


def _moe(h, gain, router, wg, wu, wd):
    t, d = h.shape
    b = MOE_BLK
    nwin = t // b
    nblk = (2 * t) // b + N_EXPERTS
    cap = nblk * b
    n_pairs = nblk + N_EXPERTS * nwin

    hn, pos, wts, cstart = _router(h, gain, router.T.astype(F32))

    cstart = cstart[:, :, 0].astype(I32).T
    routed = pos >= 0.0
    counts = jnp.sum(routed.astype(I32), axis=1)
    padded = (counts + b - 1) // b * b
    gend = jnp.cumsum(padded)
    gstart = gend - padded
    slots_rows = jnp.where(routed, pos + gstart[:, None].astype(F32), -1.0)
    slots_cols = slots_rows.T

    cend = jnp.concatenate([cstart[:, 1:], counts[:, None]], axis=1)
    lo = gstart[:, None] + cstart
    hi = gstart[:, None] + cend
    first_blk = lo // b
    n_blk = jnp.where(cend > cstart, (hi - 1) // b - first_blk + 1, 0)
    blk_d, win_d, exp_d, flag_d = _pair_list(first_blk, n_blk, True, n_pairs)
    blk_c, win_c, exp_c, flag_c = _pair_list(first_blk, n_blk, False, n_pairs)

    n_used = gend[-1] // b
    bidx = jnp.arange(nblk, dtype=I32)
    blk_used = (bidx < n_used).astype(I32)
    blk_src = jnp.minimum(bidx, n_used - 1)
    blk_expert = jnp.minimum(jnp.sum((gend[None, :] <= (blk_src * b)[:, None]).astype(I32), axis=1), N_EXPERTS - 1)

    xs, ws = _dispatch(hn, slots_rows, wts, blk_d, win_d, flag_d, exp_d, cap)
    y = _experts(xs, ws, wg, wu, wd, blk_expert, blk_src, blk_used)
    return _combine(h, y, slots_cols, win_c, blk_c, flag_c, exp_c)


def _tile_gain(gain, reps, scale=1.0):
    return (jnp.tile(gain.astype(F32), reps) * scale).reshape(1, -1)


def _mem_kv(mem2d, gain, w, k_gain):
    return _norm_proj(mem2d, gain, w.astype(BF16), _tile_gain(k_gain, XA_WIDTH // HEAD64), XA_WIDTH, 2 * XA_WIDTH)[0]


def kernel(x, mem, mix_norm, ffn_norm, mem_norm, w_mem_kv, xa_q_norm, xa_k_norm, w_out, sb_w_in, sb_q_norm, sb_k_norm, dn_w_in, dn_conv, dn_a_log, dn_dt_bias, dn_o_norm, ffd_w_gate, ffd_w_up, ffd_w_down, moe_router, moe_w_gate, moe_w_up, moe_w_down):
    batch, seq, d = x.shape
    mem_len = mem.shape[1]
    t = batch * seq
    h = x.reshape(t, d)
    mem2d = mem.reshape(batch * mem_len, d)
    scale64 = HEAD64 ** -0.5

    qk_gain = jnp.concatenate([_tile_gain(sb_q_norm[0], SB_HEADS, scale64 * LOG2E), _tile_gain(sb_k_norm[0], SB_HEADS)], axis=1)
    (proj0,) = _norm_proj(h, mix_norm[0], sb_w_in[0].astype(BF16), qk_gain, 2 * SB_WIDTH, 3 * SB_WIDTH + XA_WIDTH)
    mix0 = _sb_attention(proj0, batch, seq)
    memkv0 = _mem_kv(mem2d, mem_norm[0], w_mem_kv[0], xa_k_norm[0])
    xq_gain0 = _tile_gain(xa_q_norm[0], XA_WIDTH // HEAD64, scale64)
    h = _xattn_out(h, mix0, proj0, (3 * SB_WIDTH) // XA_WIDTH, memkv0, xq_gain0, w_out[0].astype(BF16), seq, mem_len)
    h = _ffn(h, ffn_norm[0], ffd_w_gate[0].astype(BF16), ffd_w_up[0].astype(BF16), ffd_w_down[0].astype(BF16))

    w1 = dn_w_in[0]
    n_main = 4 * DN_WIDTH + XA_WIDTH
    w1 = jnp.concatenate([w1[:, :4 * DN_WIDTH], w1[:, 4 * DN_WIDTH + 2 * DN_HEADS:], w1[:, 4 * DN_WIDTH:4 * DN_WIDTH + 2 * DN_HEADS],
                          jnp.zeros((d, LANES - 2 * DN_HEADS), F32)], axis=1).astype(BF16)
    proj1, ab = _norm_proj(h, mix_norm[1], w1, None, 0, n_main)
    pad = LANES - DN_HEADS
    alog_row = jnp.pad(dn_a_log[0].astype(F32), (0, pad)).reshape(1, LANES)
    dtb_row = jnp.pad(dn_dt_bias[0].astype(F32), (0, pad)).reshape(1, LANES)
    qkvn, gcb, gl = _dn_prep(proj1, ab, dn_conv[0].astype(F32), alog_row, dtb_row, seq)
    gct = gcb[:, :DN_HEADS].T.reshape(DN_HEADS, 1, t)
    mix1 = _delta_rule(qkvn, proj1, gcb, gl, gct, dn_o_norm[0].astype(F32).reshape(1, DN_HEAD), batch, seq)
    memkv1 = _mem_kv(mem2d, mem_norm[1], w_mem_kv[1], xa_k_norm[1])
    xq_gain1 = _tile_gain(xa_q_norm[1], XA_WIDTH // HEAD64, scale64)
    h = _xattn_out(h, mix1, proj1, (4 * DN_WIDTH) // XA_WIDTH, memkv1, xq_gain1, w_out[1].astype(BF16), seq, mem_len)
    h = _moe(h, ffn_norm[1], moe_router[0], moe_w_gate[0].astype(BF16), moe_w_up[0].astype(BF16), moe_w_down[0].astype(BF16))
    return h.reshape(batch, seq, d)
```

```python
import functools

import jax
import jax.numpy as jnp
from jax import lax
from jax.experimental import pallas as pl
from jax.experimental.pallas import tpu as pltpu

F32 = jnp.float32
BF16 = jnp.bfloat16
I32 = jnp.int32

EPS = 1e-6
LOG2E = 1.4426950408889634
D_MODEL = 1024
HEAD64 = 64
SB_HEADS = 12
SB_WIDTH = SB_HEADS * HEAD64
XA_WIDTH = 256
DN_HEADS = 6
DN_HEAD = 128
DN_WIDTH = DN_HEADS * DN_HEAD
DN_CONV = 4
DN_CHUNK = 64
D_FF = 3584
N_EXPERTS = 8

LANES = 128
MXU_DIM = 256
VMEM_LIMIT = 56 * 1024 * 1024

TM_PROJ = 512
TM_FFN = 1024
TF_FFN = 512
TF_MOE = 1792
SB_TQ = 512
SB_TK = 256
SB_ZERO_LOG2 = -160.0
DN_ROWS = 256
DN_HEADS_PER_STEP = 6
MOE_BLK = 512
MOE_PART = 256


def _cparams(*sem):
    return pltpu.CompilerParams(dimension_semantics=sem, vmem_limit_bytes=VMEM_LIMIT)


def _dot(a, b):
    return jnp.dot(a, b, preferred_element_type=F32)


def _dot_nt(a, b):
    return lax.dot_general(a, b, (((1,), (1,)), ((), ())), preferred_element_type=F32)


def _dot_tn(a, b):
    return lax.dot_general(a, b, (((0,), (0,)), ((), ())), preferred_element_type=F32)


def _rms_rows(x, gain_row):
    ms = jnp.mean(x * x, axis=-1, keepdims=True)
    return x * lax.rsqrt(ms + EPS) * gain_row


def _softplus(x):
    return jnp.maximum(x, 0.0) + jnp.log1p(jnp.exp(-jnp.abs(x)))


def _sigmoid(x):
    return 1.0 / (1.0 + jnp.exp(-x))


def _head_sum_matrix(width, head):
    r = jnp.arange(width)[:, None] // head
    c = jnp.arange(width)[None, :] // head
    return (r == c).astype(BF16)


def _norm_proj_kernel(x_ref, g_ref, w_ref, hg_ref, hm_ref, *o_refs, n_norm_cols, n_main_cols):
    xn = _rms_rows(x_ref[...], g_ref[...]).astype(BF16)
    proj = _dot(xn, w_ref[...])
    o_ref = o_refs[0]
    for c in range(n_norm_cols // MXU_DIM):
        cols = slice(c * MXU_DIM, (c + 1) * MXU_DIM)
        blk = proj[:, cols]
        ss = _dot((blk * blk).astype(BF16), hm_ref[...])
        o_ref[:, cols] = (blk * lax.rsqrt(ss * (1.0 / HEAD64) + EPS) * hg_ref[:, cols]).astype(o_ref.dtype)
    if n_main_cols > n_norm_cols:
        o_ref[:, n_norm_cols:n_main_cols] = proj[:, n_norm_cols:n_main_cols].astype(o_ref.dtype)
    if len(o_refs) > 1:
        o_refs[1][...] = proj[:, n_main_cols:]


def _norm_proj(x, gain, w, head_gain, n_norm_cols, n_main_cols):
    t, d = x.shape
    n = w.shape[1]
    tm = min(TM_PROJ, t)
    out_shape = [jax.ShapeDtypeStruct((t, n_main_cols), BF16)]
    out_specs = [pl.BlockSpec((tm, n_main_cols), lambda i: (i, 0))]
    if n > n_main_cols:
        out_shape.append(jax.ShapeDtypeStruct((t, n - n_main_cols), F32))
        out_specs.append(pl.BlockSpec((tm, n - n_main_cols), lambda i: (i, 0)))
    hg = jnp.zeros((1, max(n_norm_cols, MXU_DIM)), F32) if head_gain is None else head_gain
    res = pl.pallas_call(
        functools.partial(_norm_proj_kernel, n_norm_cols=n_norm_cols, n_main_cols=n_main_cols),
        grid=(t // tm,),
        in_specs=[
            pl.BlockSpec((tm, d), lambda i: (i, 0)),
            pl.BlockSpec((1, d), lambda i: (0, 0)),
            pl.BlockSpec((d, n), lambda i: (0, 0)),
            pl.BlockSpec(hg.shape, lambda i: (0, 0)),
            pl.BlockSpec((MXU_DIM, MXU_DIM), lambda i: (0, 0)),
        ],
        out_specs=out_specs,
        out_shape=out_shape,
        compiler_params=_cparams("parallel"),
        name="norm_proj",
    )(x, gain.reshape(1, d), w, hg, _head_sum_matrix(MXU_DIM, HEAD64))
    return res


def _sb_attn_kernel(qt_ref, k_ref, v_ref, nut_ref, o_ref):
    i = pl.program_id(2)
    tq = qt_ref.shape[1]
    tk = nut_ref.shape[0]
    qt = qt_ref[...]
    sub = lax.broadcasted_iota(I32, qt.shape, 0)
    zero = jnp.zeros_like(qt)
    qqt = jnp.concatenate([jnp.where(sub < HEAD64, qt, zero), jnp.where(sub >= HEAD64, qt, zero)], axis=1)
    nut = nut_ref[...]

    def tile_pair(p, carry, qq, masked):
        offs = [pl.multiple_of((2 * p + 1) * tk, tk), pl.multiple_of(2 * p * tk, tk)]
        zs = [_dot(k_ref[pl.ds(off, tk), :], qq) for off in offs]
        log_b, spb, causal, within, pv = [], [], [], [], []
        for n, off in enumerate(offs):
            z = zs[n]
            sp = jnp.maximum(z, jnp.log2(1.0 + jnp.exp2(jnp.minimum(z, 126.0))))
            log_b.append(z - sp)
            if masked:
                key = off + lax.broadcasted_iota(I32, z.shape, 0)
                qi = lax.broadcasted_iota(I32, z.shape, 1)
                causal.append(key < i * tq + jnp.where(qi >= tq, qi - tq, qi))
                sp = jnp.where(causal[n], sp, 0.0)
            spb.append(sp.astype(BF16))
            within.append(_dot(nut, spb[n]))
        for n, off in enumerate(offs):
            arg = log_b[n] + within[n] + carry
            if masked:
                arg = jnp.where(causal[n], arg, -1e30)
            pv.append(_dot_tn(v_ref[pl.ds(off, tk), :], jnp.exp2(arg).astype(BF16)))
            carry = carry + within[n][0:1, :] - spb[n][0:1, :].astype(F32)
        return pv[0] + pv[1], carry

    last = ((i + 1) * tq - 1) // (2 * tk)
    acc, carry = tile_pair(last, jnp.zeros((1, 2 * tq), F32), qqt, True)

    hq = tq // 2
    early = [slice(0, hq), slice(tq, tq + hq)]
    late = [slice(hq, tq), slice(tq + hq, 2 * tq)]

    def prev_full(acc, carry):
        pv, carry = tile_pair(last - 1, carry, qqt, False)
        return acc + pv, carry

    def prev_early(acc, carry):
        pv, ce = tile_pair(last - 1, jnp.concatenate([carry[:, s] for s in early], axis=1),
                           jnp.concatenate([qqt[:, s] for s in early], axis=1), False)
        acc = jnp.concatenate([acc[:, early[0]] + pv[:, :hq], acc[:, late[0]],
                               acc[:, early[1]] + pv[:, hq:], acc[:, late[1]]], axis=1)
        carry = jnp.concatenate([ce[:, :hq], carry[:, late[0]], ce[:, hq:], carry[:, late[1]]], axis=1)
        return acc, carry

    def prev_pair(acc, carry):
        late_live = jnp.max(jnp.concatenate([carry[:, s] for s in late], axis=1)) > SB_ZERO_LOG2
        return lax.cond(late_live, prev_full, prev_early, acc, carry)

    acc, carry = lax.cond(last > 0, prev_pair, lambda a, c: (a, c), acc, carry)
    rest = jnp.maximum(last - 1, 0)

    def live(st):
        return (st[0] < rest) & (jnp.max(st[2]) > SB_ZERO_LOG2)

    def body(st):
        jj, acc, carry = st
        pv, carry = tile_pair(rest - 1 - jj, carry, qqt, False)
        return jj + 1, acc + pv, carry

    _, acc, _ = lax.while_loop(live, body, (jnp.int32(0), acc, carry))
    lane = lax.broadcasted_iota(I32, (tq, LANES), 1)
    o_ref[...] = jnp.where(lane < HEAD64, acc[:, :tq].T, acc[:, tq:].T).astype(o_ref.dtype)


def _sb_attention(proj, batch, seq):
    tq = min(SB_TQ, seq)
    nq = seq // tq
    n_pairs = SB_WIDTH // LANES
    qt = proj[:, :SB_WIDTH].T
    tk = min(SB_TK, seq // 2)
    s_row = jnp.arange(tk)[:, None]
    j_col = jnp.arange(tk)[None, :]
    nut = -(j_col > s_row).astype(BF16)
    return pl.pallas_call(
        _sb_attn_kernel,
        grid=(batch, n_pairs, nq),
        in_specs=[
            pl.BlockSpec((LANES, tq), lambda b, p, i: (p, b * nq + i)),
            pl.BlockSpec((seq, LANES), lambda b, p, i: (b, n_pairs + p)),
            pl.BlockSpec((seq, LANES), lambda b, p, i: (b, 2 * n_pairs + p)),
            pl.BlockSpec((tk, tk), lambda b, p, i: (0, 0)),
        ],
        out_specs=pl.BlockSpec((tq, LANES), lambda b, p, i: (b * nq + i, p)),
        out_shape=jax.ShapeDtypeStruct((batch * seq, SB_WIDTH), BF16),
        compiler_params=_cparams("parallel", "parallel", "arbitrary"),
        name="sb_attention",
    )(qt, proj, proj, nut)


def _xattn_out_kernel(h_ref, mix_ref, xq_ref, mk_ref, mv_ref, qg_ref, hm_ref, wo_ref, o_ref):
    tm = h_ref.shape[0]
    xq = xq_ref[...].astype(F32)
    ss = _dot((xq * xq).astype(BF16), hm_ref[...])
    qn = (xq * lax.rsqrt(ss * (1.0 / HEAD64) + EPS) * qg_ref[...]).astype(BF16)
    lane = lax.broadcasted_iota(I32, (tm, XA_WIDTH), 1)
    mk = mk_ref[...]
    mv = mv_ref[...]
    xa = jnp.zeros((tm, XA_WIDTH), F32)
    for hh in range(XA_WIDTH // HEAD64):
        in_head = (lane >= hh * HEAD64) & (lane < (hh + 1) * HEAD64)
        s = _dot_nt(jnp.where(in_head, qn, jnp.zeros_like(qn)), mk)
        p = jnp.exp(s - jnp.max(s, axis=1, keepdims=True))
        o = _dot(p.astype(BF16), mv) / jnp.sum(p, axis=1, keepdims=True)
        xa = jnp.where(in_head, o, xa)
    mw = mix_ref.shape[1]
    acc = _dot(mix_ref[...], wo_ref[0:mw, :]) + _dot(xa.astype(BF16), wo_ref[mw:, :])
    o_ref[...] = h_ref[...] + acc


def _xattn_out(h, mix, proj, xq_col_block, memkv, q_gain_row, w_out, seq, mem_len):
    t, d = h.shape
    tm = min(TM_PROJ, seq)
    per_b = seq // tm
    mw = mix.shape[1]
    return pl.pallas_call(
        _xattn_out_kernel,
        grid=(t // tm,),
        in_specs=[
            pl.BlockSpec((tm, d), lambda i: (i, 0)),
            pl.BlockSpec((tm, mw), lambda i: (i, 0)),
            pl.BlockSpec((tm, XA_WIDTH), lambda i: (i, xq_col_block)),
            pl.BlockSpec((mem_len, XA_WIDTH), lambda i: (i // per_b, 0)),
            pl.BlockSpec((mem_len, XA_WIDTH), lambda i: (i // per_b, 1)),
            pl.BlockSpec((1, XA_WIDTH), lambda i: (0, 0)),
            pl.BlockSpec((XA_WIDTH, XA_WIDTH), lambda i: (0, 0)),
            pl.BlockSpec((mw + XA_WIDTH, d), lambda i: (0, 0)),
        ],
        out_specs=pl.BlockSpec((tm, d), lambda i: (i, 0)),
        out_shape=jax.ShapeDtypeStruct((t, d), F32),
        compiler_params=_cparams("parallel"),
        name="xattn_out",
    )(h, mix, proj, memkv, memkv, q_gain_row, _head_sum_matrix(XA_WIDTH, HEAD64), w_out)


def _ffn_kernel(x_ref, g_ref, wg_ref, wu_ref, wd_ref, o_ref, xn_ref, acc_ref):
    f = pl.program_id(1)

    @pl.when(f == 0)
    def _():
        xn_ref[...] = _rms_rows(x_ref[...], g_ref[...]).astype(BF16)
        acc_ref[...] = jnp.zeros_like(acc_ref)

    xn = xn_ref[...]
    g = _dot(xn, wg_ref[...])
    up = _dot(xn, wu_ref[...])
    act = (g * _sigmoid(g) * up).astype(BF16)
    acc_ref[...] += _dot(act, wd_ref[...])

    @pl.when(f == pl.num_programs(1) - 1)
    def _():
        o_ref[...] = x_ref[...] + acc_ref[...]


def _ffn(h, gain, wg, wu, wd):
    t, d = h.shape
    ff = wg.shape[1]
    tm = min(TM_FFN, t)
    tf = TF_FFN
    return pl.pallas_call(
        _ffn_kernel,
        grid=(t // tm, ff // tf),
        in_specs=[
            pl.BlockSpec((tm, d), lambda i, f: (i, 0)),
            pl.BlockSpec((1, d), lambda i, f: (0, 0)),
            pl.BlockSpec((d, tf), lambda i, f: (0, f)),
            pl.BlockSpec((d, tf), lambda i, f: (0, f)),
            pl.BlockSpec((tf, d), lambda i, f: (f, 0)),
        ],
        out_specs=pl.BlockSpec((tm, d), lambda i, f: (i, 0)),
        out_shape=jax.ShapeDtypeStruct((t, d), F32),
        scratch_shapes=[pltpu.VMEM((tm, d), BF16), pltpu.VMEM((tm, d), F32)],
        compiler_params=_cparams("parallel", "arbitrary"),
        name="ffn_dense",
    )(h, gain.reshape(1, d), wg, wu, wd)


def _dn_in_kernel(x_ref, xh_ref, g_ref, w_ref, cw_ref, alog_ref, dtb_ref, tril_ref, ones_ref,
                  qkv_ref, gx_ref, gcb_ref, gl_ref, *, per_b):
    i = pl.program_id(0)
    w3 = qkv_ref.shape[1]
    wgx = gx_ref.shape[1]
    nh = xh_ref.shape[0]
    xn = jnp.concatenate([_rms_rows(xh_ref[...], g_ref[...]).astype(BF16),
                          _rms_rows(x_ref[...], g_ref[...]).astype(BF16)], axis=0)
    proj = _dot(xn, w_ref[...])
    keep = (i % per_b != 0).astype(F32)
    halo = proj[nh - 8:nh, :w3] * keep
    proj = proj[nh:]
    x = proj[:, :w3]
    row8 = lax.broadcasted_iota(I32, (8, w3), 0)
    cw = cw_ref[...]
    acc = x * cw[DN_CONV - 1:DN_CONV, :]
    for s in range(1, DN_CONV):
        xs = pltpu.roll(x, s, 0)
        hs = pltpu.roll(halo, s, 0)
        first = jnp.where(row8 < s, hs, xs[0:8, :])
        xs = jnp.concatenate([first, xs[8:, :]], axis=0)
        acc = acc + xs * cw[DN_CONV - 1 - s:DN_CONV - s, :]
    y = acc * _sigmoid(acc)
    for c in range(3 * DN_HEADS):
        cols = slice(c * DN_HEAD, (c + 1) * DN_HEAD)
        blk = y[:, cols]
        if c < 2 * DN_HEADS:
            blk = blk * lax.rsqrt(jnp.sum(blk * blk, axis=-1, keepdims=True) + EPS)
            if c < DN_HEADS:
                blk = blk * (DN_HEAD ** -0.5)
        qkv_ref[:, cols] = blk.astype(qkv_ref.dtype)
    gx_ref[...] = proj[:, w3:w3 + wgx].astype(gx_ref.dtype)

    ab = proj[:, w3 + wgx:]
    lane = lax.broadcasted_iota(I32, ab.shape, 1)
    g = -jnp.exp(alog_ref[...]) * _softplus(ab + dtb_ref[...])
    g = jnp.where(lane < DN_HEADS, g, 0.0)
    beta = _sigmoid(ab)
    g_hi = g.astype(BF16)
    g_lo = (g - g_hi.astype(F32)).astype(BF16)
    tril = tril_ref[...]
    ones = ones_ref[...]
    gc = _dot(tril, g_hi) + _dot(tril, g_lo)
    gl = _dot(ones, g_hi) + _dot(ones, g_lo)
    gcb_ref[...] = jnp.where(lane < DN_HEADS, gc, beta)
    gl_ref[...] = gl


def _dn_in(h, gain, w, conv_w, alog_row, dtb_row, seq):
    t, d = h.shape
    tm = min(TM_PROJ, seq)
    per_b = seq // tm
    w3 = 3 * DN_WIDTH
    wgx = DN_WIDTH + XA_WIDTH
    hb = 16
    r = jnp.arange(tm)[:, None]
    c = jnp.arange(tm)[None, :]
    same = (r // DN_CHUNK) == (c // DN_CHUNK)
    tril = (same & (c <= r)).astype(BF16)
    ones = same.astype(BF16)
    return pl.pallas_call(
        functools.partial(_dn_in_kernel, per_b=per_b),
        grid=(t // tm,),
        in_specs=[
            pl.BlockSpec((tm, d), lambda i: (i, 0)),
            pl.BlockSpec((hb, d), lambda i: (jnp.maximum(i * (tm // hb) - 1, 0), 0)),
            pl.BlockSpec((1, d), lambda i: (0, 0)),
            pl.BlockSpec(w.shape, lambda i: (0, 0)),
            pl.BlockSpec((DN_CONV, w3), lambda i: (0, 0)),
            pl.BlockSpec((1, LANES), lambda i: (0, 0)),
            pl.BlockSpec((1, LANES), lambda i: (0, 0)),
            pl.BlockSpec((tm, tm), lambda i: (0, 0)),
            pl.BlockSpec((tm, tm), lambda i: (0, 0)),
        ],
        out_specs=[
            pl.BlockSpec((tm, w3), lambda i: (i, 0)),
            pl.BlockSpec((tm, wgx), lambda i: (i, 0)),
            pl.BlockSpec((tm, LANES), lambda i: (i, 0)),
            pl.BlockSpec((tm, LANES), lambda i: (i, 0)),
        ],
        out_shape=[
            jax.ShapeDtypeStruct((t, w3), BF16),
            jax.ShapeDtypeStruct((t, wgx), BF16),
            jax.ShapeDtypeStruct((t, LANES), F32),
            jax.ShapeDtypeStruct((t, LANES), F32),
        ],
        compiler_params=_cparams("parallel"),
        name="dn_in",
    )(h, h, gain.reshape(1, d), w, conv_w, alog_row, dtb_row, tril, ones)


def _delta_kernel(q_ref, k_ref, v_ref, gate_ref, gcb_ref, gl_ref, gct_ref, og_ref, o_ref, s_ref, *, heads):
    h0 = pl.program_id(1) * heads
    rows = q_ref.shape[0]
    n_chunks = rows // DN_CHUNK
    hs = range(heads)

    @pl.when(pl.program_id(2) == 0)
    def _():
        s_ref[...] = jnp.zeros_like(s_ref)

    lane = lax.broadcasted_iota(I32, (rows, LANES), 1)
    gcb = gcb_ref[...]
    glb = gl_ref[...]
    gc_col = [jnp.sum(jnp.where(lane == h0 + h, gcb, 0.0), axis=1, keepdims=True) for h in hs]
    beta_col = [jnp.sum(jnp.where(lane == h0 + h + DN_HEADS, gcb, 0.0), axis=1, keepdims=True) for h in hs]
    gl_col = [jnp.sum(jnp.where(lane == h0 + h, glb, 0.0), axis=1, keepdims=True) for h in hs]

    ri = lax.broadcasted_iota(I32, (rows, rows), 0)
    ci = lax.broadcasted_iota(I32, (rows, rows), 1)
    same = (ri // DN_CHUNK) == (ci // DN_CHUNK)
    tri = same & (ci <= ri)
    strict = same & (ci < ri)
    eye = (ri == ci).astype(F32)
    cols = [slice(h * DN_HEAD, (h + 1) * DN_HEAD) for h in hs]

    decay = [jnp.exp(jnp.where(tri, gc_col[h] - gct_ref[h], -1e30)) for h in hs]
    k = [k_ref[:, cols[h]] for h in hs]
    q = [q_ref[:, cols[h]] for h in hs]
    kf = [k[h].astype(F32) for h in hs]
    eg = [jnp.exp(gc_col[h]) for h in hs]
    kb = [kf[h] * beta_col[h] for h in hs]
    rhs = [jnp.concatenate([v_ref[:, cols[h]].astype(F32) * beta_col[h], kb[h] * eg[h]], axis=1).astype(BF16) for h in hs]
    npow = [jnp.where(strict, -(_dot_nt(kb[h].astype(BF16), k[h]) * decay[h]), 0.0) for h in hs]
    inv = [eye + npow[h] for h in hs]
    for _ in range(5):
        nb = [npow[h].astype(BF16) for h in hs]
        npow = [_dot(nb[h], nb[h]) for h in hs]
        inv = [inv[h] + _dot(npow[h].astype(BF16), inv[h].astype(BF16)) for h in hs]
    uw = [_dot(inv[h].astype(BF16), rhs[h]) for h in hs]
    qk = [(_dot_nt(q[h], k[h]) * decay[h]).astype(BF16) for h in hs]
    w_b = [uw[h][:, DN_HEAD:].astype(BF16) for h in hs]
    q_dec = [(q[h].astype(F32) * eg[h]).astype(BF16) for h in hs]
    k_dec = [(kf[h] * jnp.exp(gl_col[h] - gc_col[h])).astype(BF16) for h in hs]

    state = [s_ref[h] for h in hs]
    outs = [[] for _ in hs]
    for c in range(n_chunks):
        rs = slice(c * DN_CHUNK, (c + 1) * DN_CHUNK)
        sb = [state[h].astype(BF16) for h in hs]
        ws = [_dot(jnp.concatenate([w_b[h][rs], q_dec[h][rs]], axis=0), sb[h]) for h in hs]
        v_new = [(uw[h][rs, :DN_HEAD] - ws[h][:DN_CHUNK]).astype(BF16) for h in hs]
        for h in hs:
            outs[h].append(ws[h][DN_CHUNK:] + _dot(qk[h][rs, rs], v_new[h]))
        state = [state[h] * jnp.exp(gl_col[h][c * DN_CHUNK:c * DN_CHUNK + 1, :]) + _dot_tn(k_dec[h][rs], v_new[h]) for h in hs]
    for h in hs:
        s_ref[h] = state[h]
        o = jnp.concatenate(outs[h], axis=0)
        gate = gate_ref[:, cols[h]].astype(F32)
        o_ref[:, cols[h]] = (_rms_rows(o, og_ref[...]) * (gate * _sigmoid(gate))).astype(o_ref.dtype)


def _delta_rule(qkvn, gx, gcb, gl, gct, o_gain_row, batch, seq):
    rows = min(DN_ROWS, seq)
    steps = seq // rows
    t = batch * seq
    heads = DN_HEADS_PER_STEP
    groups = DN_HEADS // heads
    w = heads * DN_HEAD
    return pl.pallas_call(
        functools.partial(_delta_kernel, heads=heads),
        grid=(batch, groups, steps),
        in_specs=[
            pl.BlockSpec((rows, w), lambda b, g, s: (b * steps + s, g)),
            pl.BlockSpec((rows, w), lambda b, g, s: (b * steps + s, groups + g)),
            pl.BlockSpec((rows, w), lambda b, g, s: (b * steps + s, 2 * groups + g)),
            pl.BlockSpec((rows, w), lambda b, g, s: (b * steps + s, g)),
            pl.BlockSpec((rows, LANES), lambda b, g, s: (b * steps + s, 0)),
            pl.BlockSpec((rows, LANES), lambda b, g, s: (b * steps + s, 0)),
            pl.BlockSpec((heads, 1, rows), lambda b, g, s: (g, 0, b * steps + s)),
            pl.BlockSpec((1, DN_HEAD), lambda b, g, s: (0, 0)),
        ],
        out_specs=pl.BlockSpec((rows, w), lambda b, g, s: (b * steps + s, g)),
        out_shape=jax.ShapeDtypeStruct((t, DN_WIDTH), BF16),
        scratch_shapes=[pltpu.VMEM((heads, DN_HEAD, DN_HEAD), F32)],
        compiler_params=_cparams("parallel", "parallel", "arbitrary"),
        name="delta_rule",
    )(qkvn, qkvn, qkvn, gx, gcb, gl, gct, o_gain_row)


def _router_kernel(h_ref, g_ref, rt_ref, us_ref, hn_ref, pos_ref, wts_ref, cstart_ref, carry_ref):
    i = pl.program_id(0)

    @pl.when(i == 0)
    def _():
        carry_ref[...] = jnp.zeros_like(carry_ref)

    hn = _rms_rows(h_ref[...], g_ref[...])
    hn_ref[...] = hn.astype(BF16)
    logits = lax.dot_general(rt_ref[...], hn, (((1,), (1,)), ((), ())),
                             precision=lax.Precision.HIGHEST, preferred_element_type=F32)
    sub = lax.broadcasted_iota(I32, logits.shape, 0)
    m1 = jnp.max(logits, axis=0, keepdims=True)
    i1 = jnp.min(jnp.where(logits == m1, sub, N_EXPERTS), axis=0, keepdims=True)
    rest = jnp.where(sub == i1, -jnp.inf, logits)
    m2 = jnp.max(rest, axis=0, keepdims=True)
    i2 = jnp.min(jnp.where(rest == m2, sub, N_EXPERTS), axis=0, keepdims=True)
    e21 = jnp.exp(m2 - m1)
    w1 = 1.0 / (1.0 + e21)
    w2 = e21 / (1.0 + e21)
    hit1 = sub == i1
    hit2 = sub == i2
    chosen = hit1 | hit2
    onehot = jnp.where(chosen, 1.0, 0.0)
    carry = carry_ref[...]
    cstart_ref[0] = carry
    before = _dot(onehot.astype(BF16), us_ref[...]) + carry[:, 0:1]
    carry_ref[...] = carry + jnp.sum(onehot, axis=1, keepdims=True)
    pos_ref[...] = jnp.where(chosen, before, -1.0)
    wts_ref[...] = jnp.where(hit1, w1, jnp.where(hit2, w2, 0.0))


def _router(h, gain, router_t):
    t, d = h.shape
    tm = min(MOE_BLK, t)
    nwin = t // tm
    r = jnp.arange(tm)[:, None]
    c = jnp.arange(tm)[None, :]
    us = (r < c).astype(BF16)
    return pl.pallas_call(
        _router_kernel,
        grid=(nwin,),
        in_specs=[
            pl.BlockSpec((tm, d), lambda i: (i, 0)),
            pl.BlockSpec((1, d), lambda i: (0, 0)),
            pl.BlockSpec((N_EXPERTS, d), lambda i: (0, 0)),
            pl.BlockSpec((tm, tm), lambda i: (0, 0)),
        ],
        out_specs=[
            pl.BlockSpec((tm, d), lambda i: (i, 0)),
            pl.BlockSpec((N_EXPERTS, tm), lambda i: (0, i)),
            pl.BlockSpec((N_EXPERTS, tm), lambda i: (0, i)),
            pl.BlockSpec((1, N_EXPERTS, LANES), lambda i: (i, 0, 0)),
        ],
        out_shape=[
            jax.ShapeDtypeStruct((t, d), BF16),
            jax.ShapeDtypeStruct((N_EXPERTS, t), F32),
            jax.ShapeDtypeStruct((N_EXPERTS, t), F32),
            jax.ShapeDtypeStruct((nwin, N_EXPERTS, LANES), F32),
        ],
        scratch_shapes=[pltpu.VMEM((N_EXPERTS, LANES), F32)],
        compiler_params=_cparams("arbitrary"),
        name="moe_router",
    )(h, gain.reshape(1, d), router_t, us)


def _dispatch_kernel(blk_ref, win_ref, flag_ref, exp_ref, lo_ref, hi_ref, hn_ref, slots_ref, wts_ref, xs_ref, ws_ref):
    l = pl.program_id(0)
    flag = flag_ref[l]
    nslot = xs_ref.shape[0]

    @pl.when((flag & 2) != 0)
    def _():
        xs_ref[...] = jnp.zeros_like(xs_ref)
        ws_ref[...] = jnp.zeros_like(ws_ref)

    for part in range(nslot // MOE_PART):
        r0 = part * MOE_PART

        @pl.when(((flag & 1) != 0) & (lo_ref[l] < r0 + MOE_PART) & (hi_ref[l] > r0))
        def _():
            ntok = hn_ref.shape[0]
            e = exp_ref[l]
            base = (blk_ref[l] * nslot + r0).astype(F32)
            rel = slots_ref[pl.ds(e, 1), :] - base
            hit = rel == lax.broadcasted_iota(I32, (MOE_PART, ntok), 0).astype(F32)
            rows = pl.ds(r0, MOE_PART)
            xs_ref[rows, :] += _dot(jnp.where(hit, 1.0, 0.0).astype(BF16), hn_ref[...]).astype(xs_ref.dtype)
            ws_ref[rows, :] += jnp.sum(jnp.where(hit, wts_ref[pl.ds(e, 1), :], 0.0), axis=1, keepdims=True)


def _dispatch(hn, slots_rows, wts, blk, win, flag, exp, lo, hi, cap):
    t, d = hn.shape
    b = MOE_BLK
    n_pairs = blk.shape[0]
    return pl.pallas_call(
        _dispatch_kernel,
        grid_spec=pltpu.PrefetchScalarGridSpec(
            num_scalar_prefetch=6,
            grid=(n_pairs,),
            in_specs=[
                pl.BlockSpec((b, d), lambda l, blk, win, *_: (win[l], 0)),
                pl.BlockSpec((N_EXPERTS, b), lambda l, blk, win, *_: (0, win[l])),
                pl.BlockSpec((N_EXPERTS, b), lambda l, blk, win, *_: (0, win[l])),
            ],
            out_specs=[
                pl.BlockSpec((b, d), lambda l, blk, win, *_: (blk[l], 0)),
                pl.BlockSpec((b, LANES), lambda l, blk, win, *_: (blk[l], 0)),
            ],
        ),
        out_shape=[jax.ShapeDtypeStruct((cap, d), BF16), jax.ShapeDtypeStruct((cap, LANES), F32)],
        compiler_params=_cparams("arbitrary"),
        name="moe_dispatch",
    )(blk, win, flag, exp, lo, hi, hn, slots_rows, wts)


def _experts_kernel(be_ref, src_ref, used_ref, xs_ref, ws_ref, wg_ref, wu_ref, wd_ref, y_ref, acc_ref):
    b = pl.program_id(0)
    f = pl.program_id(1)
    used = used_ref[b] != 0

    @pl.when(f == 0)
    def _():
        acc_ref[...] = jnp.zeros_like(acc_ref)

    @pl.when(used)
    def _():
        xs = xs_ref[...]
        g = _dot(xs, wg_ref[...])
        up = _dot(xs, wu_ref[...])
        act = (g * _sigmoid(g) * up).astype(BF16)
        acc_ref[...] += _dot(act, wd_ref[...])

    @pl.when(f == pl.num_programs(1) - 1)
    def _():
        y_ref[...] = jnp.where(used, acc_ref[...] * ws_ref[:, 0:1], 0.0).astype(y_ref.dtype)


def _experts(xs, ws, wg, wu, wd, blk_expert, blk_src, blk_used):
    cap, d = xs.shape
    ff = wg.shape[2]
    b = MOE_BLK
    tf = TF_MOE
    return pl.pallas_call(
        _experts_kernel,
        grid_spec=pltpu.PrefetchScalarGridSpec(
            num_scalar_prefetch=3,
            grid=(cap // b, ff // tf),
            in_specs=[
                pl.BlockSpec((b, d), lambda i, f, be, src, used: (src[i], 0)),
                pl.BlockSpec((b, LANES), lambda i, f, be, src, used: (src[i], 0)),
                pl.BlockSpec((None, d, tf), lambda i, f, be, src, used: (be[i], 0, f)),
                pl.BlockSpec((None, d, tf), lambda i, f, be, src, used: (be[i], 0, f)),
                pl.BlockSpec((None, tf, d), lambda i, f, be, src, used: (be[i], f, 0)),
            ],
            out_specs=pl.BlockSpec((b, d), lambda i, f, be, src, used: (i, 0)),
            scratch_shapes=[pltpu.VMEM((b, d), F32)],
        ),
        out_shape=jax.ShapeDtypeStruct((cap, d), BF16),
        compiler_params=_cparams("parallel", "arbitrary"),
        name="moe_experts",
    )(blk_expert, blk_src, blk_used, xs, ws, wg, wu, wd)


def _combine_kernel(win_ref, blk_ref, flag_ref, exp_ref, lo_ref, hi_ref, h_ref, y_ref, slots_ref, o_ref):
    l = pl.program_id(0)
    flag = flag_ref[l]

    @pl.when((flag & 2) != 0)
    def _():
        o_ref[...] = h_ref[...]

    nslot = y_ref.shape[0]
    half = nslot // 2
    for part in range(2):
        c0 = part * half

        @pl.when(((flag & 1) != 0) & (lo_ref[l] < c0 + half) & (hi_ref[l] > c0))
        def _():
            ntok = h_ref.shape[0]
            cols = slots_ref[...]
            mine = lax.broadcasted_iota(I32, cols.shape, 1) == exp_ref[l]
            base = (blk_ref[l] * nslot + c0).astype(F32)
            rel = jnp.sum(jnp.where(mine, cols, 0.0), axis=1, keepdims=True) - base
            hit = rel == lax.broadcasted_iota(I32, (ntok, half), 1).astype(F32)
            o_ref[...] += _dot(jnp.where(hit, 1.0, 0.0).astype(BF16), y_ref[pl.ds(c0, half), :])


def _combine(h, y, slots_cols, win, blk, flag, exp, lo, hi):
    t, d = h.shape
    b = MOE_BLK
    n_pairs = win.shape[0]
    return pl.pallas_call(
        _combine_kernel,
        grid_spec=pltpu.PrefetchScalarGridSpec(
            num_scalar_prefetch=6,
            grid=(n_pairs,),
            in_specs=[
                pl.BlockSpec((b, d), lambda l, win, blk, *_: (win[l], 0)),
                pl.BlockSpec((b, d), lambda l, win, blk, *_: (blk[l], 0)),
                pl.BlockSpec((b, N_EXPERTS), lambda l, win, blk, *_: (win[l], 0)),
            ],
            out_specs=pl.BlockSpec((b, d), lambda l, win, blk, *_: (win[l], 0)),
        ),
        out_shape=jax.ShapeDtypeStruct((t, d), F32),
        compiler_params=_cparams("arbitrary"),
        name="moe_combine",
    )(win, blk, flag, exp, lo, hi, h, y, slots_cols)


def _pair_list(first_blk, n_blk, slot_lo, slot_hi, expert_major, n_pairs):
    n_e, n_w = first_blk.shape
    order = (lambda a: a.reshape(-1)) if expert_major else (lambda a: a.T.reshape(-1))
    if expert_major:
        fb, nb = first_blk.reshape(-1), n_blk.reshape(-1)
        win_of = jnp.tile(jnp.arange(n_w, dtype=I32), n_e)
        exp_of = jnp.repeat(jnp.arange(n_e, dtype=I32), n_w)
    else:
        fb, nb = first_blk.T.reshape(-1), n_blk.T.reshape(-1)
        win_of = jnp.repeat(jnp.arange(n_w, dtype=I32), n_e)
        exp_of = jnp.tile(jnp.arange(n_e, dtype=I32), n_w)
    end = jnp.cumsum(nb)
    total = end[-1]
    l = jnp.arange(n_pairs, dtype=I32)
    lc = jnp.minimum(l, total - 1)
    p = jnp.sum((end[None, :] <= lc[:, None]).astype(I32), axis=1)
    onehot = jnp.arange(fb.shape[0], dtype=I32)[None, :] == p[:, None]

    def pick(v):
        return jnp.sum(jnp.where(onehot, v[None, :], 0), axis=1)

    blk = pick(fb) + (lc - (pick(end) - pick(nb)))
    win = pick(win_of)
    major = blk if expert_major else win
    first = jnp.concatenate([jnp.ones((1,), bool), major[1:] != major[:-1]])
    valid = l < total
    flag = valid.astype(I32) + 2 * (first & valid).astype(I32)
    lo = jnp.clip(pick(order(slot_lo)) - blk * MOE_BLK, 0, MOE_BLK)
    hi = jnp.clip(pick(order(slot_hi)) - blk * MOE_BLK, 0, MOE_BLK)
    return blk, win, pick(exp_of), flag, lo, hi


def _moe(h, gain, router, wg, wu, wd):
    t, d = h.shape
    b = MOE_BLK
    nwin = t // b
    nblk = (2 * t) // b + N_EXPERTS
    cap = nblk * b
    n_pairs = nblk + N_EXPERTS * nwin

    hn, pos, wts, cstart = _router(h, gain, router.T.astype(F32))

    cstart = cstart[:, :, 0].astype(I32).T
    routed = pos >= 0.0
    counts = jnp.sum(routed.astype(I32), axis=1)
    padded = (counts + b - 1) // b * b
    gend = jnp.cumsum(padded)
    gstart = gend - padded
    slots_rows = jnp.where(routed, pos + gstart[:, None].astype(F32), -1.0)
    slots_cols = slots_rows.T

    cend = jnp.concatenate([cstart[:, 1:], counts[:, None]], axis=1)
    lo = gstart[:, None] + cstart
    hi = gstart[:, None] + cend
    first_blk = lo // b
    n_blk = jnp.where(cend > cstart, (hi - 1) // b - first_blk + 1, 0)
    blk_d, win_d, exp_d, flag_d, lo_d, hi_d = _pair_list(first_blk, n_blk, lo, hi, True, n_pairs)
    blk_c, win_c, exp_c, flag_c, lo_c, hi_c = _pair_list(first_blk, n_blk, lo, hi, False, n_pairs)

    n_used = gend[-1] // b
    bidx = jnp.arange(nblk, dtype=I32)
    blk_used = (bidx < n_used).astype(I32)
    blk_src = jnp.minimum(bidx, n_used - 1)
    blk_expert = jnp.minimum(jnp.sum((gend[None, :] <= (blk_src * b)[:, None]).astype(I32), axis=1), N_EXPERTS - 1)

    xs, ws = _dispatch(hn, slots_rows, wts, blk_d, win_d, flag_d, exp_d, lo_d, hi_d, cap)
    y = _experts(xs, ws, wg, wu, wd, blk_expert, blk_src, blk_used)
    return _combine(h, y, slots_cols, win_c, blk_c, flag_c, exp_c, lo_c, hi_c)


def _tile_gain(gain, reps, scale=1.0):
    return (jnp.tile(gain.astype(F32), reps) * scale).reshape(1, -1)


def _mem_kv(mem2d, gain, w, k_gain):
    return _norm_proj(mem2d, gain, w.astype(BF16), _tile_gain(k_gain, XA_WIDTH // HEAD64), XA_WIDTH, 2 * XA_WIDTH)[0]


def kernel(x, mem, mix_norm, ffn_norm, mem_norm, w_mem_kv, xa_q_norm, xa_k_norm, w_out, sb_w_in, sb_q_norm, sb_k_norm, dn_w_in, dn_conv, dn_a_log, dn_dt_bias, dn_o_norm, ffd_w_gate, ffd_w_up, ffd_w_down, moe_router, moe_w_gate, moe_w_up, moe_w_down):
    batch, seq, d = x.shape
    mem_len = mem.shape[1]
    t = batch * seq
    h = x.reshape(t, d)
    mem2d = mem.reshape(batch * mem_len, d)
    scale64 = HEAD64 ** -0.5

    qk_gain = jnp.concatenate([_tile_gain(sb_q_norm[0], SB_HEADS, scale64 * LOG2E), _tile_gain(sb_k_norm[0], SB_HEADS)], axis=1)
    (proj0,) = _norm_proj(h, mix_norm[0], sb_w_in[0].astype(BF16), qk_gain, 2 * SB_WIDTH, 3 * SB_WIDTH + XA_WIDTH)
    mix0 = _sb_attention(proj0, batch, seq)
    memkv0 = _mem_kv(mem2d, mem_norm[0], w_mem_kv[0], xa_k_norm[0])
    xq_gain0 = _tile_gain(xa_q_norm[0], XA_WIDTH // HEAD64, scale64)
    h = _xattn_out(h, mix0, proj0, (3 * SB_WIDTH) // XA_WIDTH, memkv0, xq_gain0, w_out[0].astype(BF16), seq, mem_len)
    h = _ffn(h, ffn_norm[0], ffd_w_gate[0].astype(BF16), ffd_w_up[0].astype(BF16), ffd_w_down[0].astype(BF16))

    w1 = dn_w_in[0]
    w1 = jnp.concatenate([w1[:, :4 * DN_WIDTH], w1[:, 4 * DN_WIDTH + 2 * DN_HEADS:], w1[:, 4 * DN_WIDTH:4 * DN_WIDTH + 2 * DN_HEADS],
                          jnp.zeros((d, LANES - 2 * DN_HEADS), F32)], axis=1).astype(BF16)
    pad = LANES - DN_HEADS
    alog_row = jnp.pad(dn_a_log[0].astype(F32), (0, pad)).reshape(1, LANES)
    dtb_row = jnp.pad(dn_dt_bias[0].astype(F32), (0, pad)).reshape(1, LANES)
    qkvn, gx, gcb, gl = _dn_in(h, mix_norm[1], w1, dn_conv[0].astype(F32), alog_row, dtb_row, seq)
    gct = gcb[:, :DN_HEADS].T.reshape(DN_HEADS, 1, t)
    mix1 = _delta_rule(qkvn, gx, gcb, gl, gct, dn_o_norm[0].astype(F32).reshape(1, DN_HEAD), batch, seq)
    memkv1 = _mem_kv(mem2d, mem_norm[1], w_mem_kv[1], xa_k_norm[1])
    xq_gain1 = _tile_gain(xa_q_norm[1], XA_WIDTH // HEAD64, scale64)
    h = _xattn_out(h, mix1, gx, DN_WIDTH // XA_WIDTH, memkv1, xq_gain1, w_out[1].astype(BF16), seq, mem_len)
    h = _moe(h, ffn_norm[1], moe_router[0], moe_w_gate[0].astype(BF16), moe_w_up[0].astype(BF16), moe_w_down[0].astype(BF16))
    return h.reshape(batch, seq, d)
```

```python
import functools

import jax
import jax.numpy as jnp
from jax import lax
from jax.experimental import pallas as pl
from jax.experimental.pallas import tpu as pltpu

F32 = jnp.float32
BF16 = jnp.bfloat16
I32 = jnp.int32

EPS = 1e-6
LOG2E = 1.4426950408889634
D_MODEL = 1024
HEAD64 = 64
SB_HEADS = 12
SB_WIDTH = SB_HEADS * HEAD64
XA_WIDTH = 256
DN_HEADS = 6
DN_HEAD = 128
DN_WIDTH = DN_HEADS * DN_HEAD
DN_CONV = 4
DN_CHUNK = 64
D_FF = 3584
N_EXPERTS = 8

LANES = 128
MXU_DIM = 256
VMEM_LIMIT = 56 * 1024 * 1024

TM_PROJ = 512
TM_FFN = 1024
TF_FFN = 512
TF_MOE = 1792
SB_TQ = 512
SB_TK = 256
SB_ZERO_LOG2 = -160.0
DN_ROWS = 256
DN_HEADS_PER_STEP = 6
MOE_BLK = 512
MOE_PART = 256


def _cparams(*sem):
    return pltpu.CompilerParams(dimension_semantics=sem, vmem_limit_bytes=VMEM_LIMIT)


def _dot(a, b):
    return jnp.dot(a, b, preferred_element_type=F32)


def _dot_nt(a, b):
    return lax.dot_general(a, b, (((1,), (1,)), ((), ())), preferred_element_type=F32)


def _dot_tn(a, b):
    return lax.dot_general(a, b, (((0,), (0,)), ((), ())), preferred_element_type=F32)


def _rms_rows(x, gain_row):
    ms = jnp.mean(x * x, axis=-1, keepdims=True)
    return x * lax.rsqrt(ms + EPS) * gain_row


def _softplus(x):
    return jnp.maximum(x, 0.0) + jnp.log1p(jnp.exp(-jnp.abs(x)))


def _sigmoid(x):
    return 1.0 / (1.0 + jnp.exp(-x))


def _head_sum_matrix(width, head):
    r = jnp.arange(width)[:, None] // head
    c = jnp.arange(width)[None, :] // head
    return (r == c).astype(BF16)


def _norm_proj_kernel(x_ref, g_ref, w_ref, hg_ref, hm_ref, o_ref, *t_refs, n_norm_cols):
    xn = _rms_rows(x_ref[...], g_ref[...]).astype(BF16)
    proj = _dot(xn, w_ref[...])
    n_t_cols = t_refs[0].shape[0] if t_refs else 0
    for c in range(n_norm_cols // MXU_DIM):
        cols = slice(c * MXU_DIM, (c + 1) * MXU_DIM)
        blk = proj[:, cols]
        ss = _dot((blk * blk).astype(BF16), hm_ref[...])
        blk = blk * lax.rsqrt(ss * (1.0 / HEAD64) + EPS) * hg_ref[:, cols]
        o_ref[:, cols] = blk.astype(o_ref.dtype)
        if c * MXU_DIM < n_t_cols:
            t_refs[0][cols, :] = blk.T.astype(t_refs[0].dtype)
    o_ref[:, n_norm_cols:] = proj[:, n_norm_cols:].astype(o_ref.dtype)


def _norm_proj(x, gain, w, head_gain, n_norm_cols, n_t_cols=0):
    t, d = x.shape
    n = w.shape[1]
    tm = min(TM_PROJ, t)
    out_shape = [jax.ShapeDtypeStruct((t, n), BF16)]
    out_specs = [pl.BlockSpec((tm, n), lambda i: (i, 0))]
    if n_t_cols:
        out_shape.append(jax.ShapeDtypeStruct((n_t_cols, t), BF16))
        out_specs.append(pl.BlockSpec((n_t_cols, tm), lambda i: (0, i)))
    hg = head_gain
    res = pl.pallas_call(
        functools.partial(_norm_proj_kernel, n_norm_cols=n_norm_cols),
        grid=(t // tm,),
        in_specs=[
            pl.BlockSpec((tm, d), lambda i: (i, 0)),
            pl.BlockSpec((1, d), lambda i: (0, 0)),
            pl.BlockSpec((d, n), lambda i: (0, 0)),
            pl.BlockSpec(hg.shape, lambda i: (0, 0)),
            pl.BlockSpec((MXU_DIM, MXU_DIM), lambda i: (0, 0)),
        ],
        out_specs=out_specs,
        out_shape=out_shape,
        compiler_params=_cparams("parallel"),
        name="norm_proj",
    )(x, gain.reshape(1, d), w, hg, _head_sum_matrix(MXU_DIM, HEAD64))
    return res


def _sb_attn_kernel(qt_ref, k_ref, v_ref, nut_ref, o_ref):
    i = pl.program_id(2)
    tq = qt_ref.shape[1]
    tk = nut_ref.shape[0]
    qt = qt_ref[...]
    sub = lax.broadcasted_iota(I32, qt.shape, 0)
    zero = jnp.zeros_like(qt)
    qqt = jnp.concatenate([jnp.where(sub < HEAD64, qt, zero), jnp.where(sub >= HEAD64, qt, zero)], axis=1)
    nut = nut_ref[...]

    def scores(off, qq):
        return _dot(k_ref[pl.ds(off, tk), :], qq)

    def logits(z, off, qpos):
        sp = jnp.maximum(z, jnp.log2(1.0 + jnp.exp2(jnp.minimum(z, 126.0))))
        log_b = z - sp
        causal = None
        if qpos is not None:
            causal = off + lax.broadcasted_iota(I32, z.shape, 0) < qpos
            sp = jnp.where(causal, sp, 0.0)
        spb = sp.astype(BF16)
        return log_b, spb, causal, _dot(nut, spb)

    def values(off, log_b, spb, causal, within, carry):
        arg = log_b + within + carry
        if causal is not None:
            arg = jnp.where(causal, arg, -1e30)
        pv = _dot_tn(v_ref[pl.ds(off, tk), :], jnp.exp2(arg).astype(BF16))
        return pv, carry + within[0:1, :] - spb[0:1, :].astype(F32)

    def tile_pair(p, carry, qq):
        offs = [pl.multiple_of((2 * p + 1) * tk, tk), pl.multiple_of(2 * p * tk, tk)]
        zs = [scores(off, qq) for off in offs]
        parts = [logits(z, off, None) for z, off in zip(zs, offs)]
        pv_hi, carry = values(offs[0], *parts[0], carry)
        pv_lo, carry = values(offs[1], *parts[1], carry)
        return pv_hi + pv_lo, carry

    hq = tq // 2
    early = [slice(0, hq), slice(tq, tq + hq)]
    late = [slice(hq, tq), slice(tq + hq, 2 * tq)]
    last = ((i + 1) * tq - 1) // (2 * tk)
    qi = lax.broadcasted_iota(I32, (1, 2 * tq), 1)
    qpos = i * tq + jnp.where(qi >= tq, qi - tq, qi)

    off_hi = pl.multiple_of((2 * last + 1) * tk, tk)
    off_lo = pl.multiple_of(2 * last * tk, tk)
    z_hi = scores(off_hi, jnp.concatenate([qqt[:, s] for s in late], axis=1))
    z_lo = scores(off_lo, qqt)
    part_hi = logits(z_hi, off_hi, jnp.concatenate([qpos[:, s] for s in late], axis=1))
    part_lo = logits(z_lo, off_lo, qpos)
    pv_hi, c_late = values(off_hi, *part_hi, jnp.zeros((1, tq), F32))
    zeros_h = jnp.zeros((1, hq), F32)
    pv_lo, carry = values(off_lo, *part_lo, jnp.concatenate([zeros_h, c_late[:, :hq], zeros_h, c_late[:, hq:]], axis=1))
    acc = jnp.concatenate([pv_lo[:, early[0]], pv_lo[:, late[0]] + pv_hi[:, :hq],
                           pv_lo[:, early[1]], pv_lo[:, late[1]] + pv_hi[:, hq:]], axis=1)


    def prev_full(acc, carry):
        pv, carry = tile_pair(last - 1, carry, qqt)
        return acc + pv, carry

    def prev_early(acc, carry):
        pv, ce = tile_pair(last - 1, jnp.concatenate([carry[:, s] for s in early], axis=1),
                           jnp.concatenate([qqt[:, s] for s in early], axis=1))
        acc = jnp.concatenate([acc[:, early[0]] + pv[:, :hq], acc[:, late[0]],
                               acc[:, early[1]] + pv[:, hq:], acc[:, late[1]]], axis=1)
        carry = jnp.concatenate([ce[:, :hq], carry[:, late[0]], ce[:, hq:], carry[:, late[1]]], axis=1)
        return acc, carry

    def prev_pair(acc, carry):
        late_live = jnp.max(jnp.concatenate([carry[:, s] for s in late], axis=1)) > SB_ZERO_LOG2
        return lax.cond(late_live, prev_full, prev_early, acc, carry)

    acc, carry = lax.cond(last > 0, prev_pair, lambda a, c: (a, c), acc, carry)
    rest = jnp.maximum(last - 1, 0)

    def live(st):
        return (st[0] < rest) & (jnp.max(st[2]) > SB_ZERO_LOG2)

    def body(st):
        jj, acc, carry = st
        pv, carry = tile_pair(rest - 1 - jj, carry, qqt)
        return jj + 1, acc + pv, carry

    _, acc, _ = lax.while_loop(live, body, (jnp.int32(0), acc, carry))
    lane = lax.broadcasted_iota(I32, (tq, LANES), 1)
    o_ref[...] = jnp.where(lane < HEAD64, acc[:, :tq].T, acc[:, tq:].T).astype(o_ref.dtype)


def _sb_attention(proj, qt, batch, seq):
    tq = min(SB_TQ, seq)
    nq = seq // tq
    n_pairs = SB_WIDTH // LANES
    tk = min(SB_TK, seq // 2)
    assert tq == 2 * tk, "the diagonal step covers the query tile with exactly two key tiles"
    s_row = jnp.arange(tk)[:, None]
    j_col = jnp.arange(tk)[None, :]
    nut = -(j_col > s_row).astype(BF16)
    return pl.pallas_call(
        _sb_attn_kernel,
        grid=(batch, n_pairs, nq),
        in_specs=[
            pl.BlockSpec((LANES, tq), lambda b, p, i: (p, b * nq + i)),
            pl.BlockSpec((seq, LANES), lambda b, p, i: (b, n_pairs + p)),
            pl.BlockSpec((seq, LANES), lambda b, p, i: (b, 2 * n_pairs + p)),
            pl.BlockSpec((tk, tk), lambda b, p, i: (0, 0)),
        ],
        out_specs=pl.BlockSpec((tq, LANES), lambda b, p, i: (b * nq + i, p)),
        out_shape=jax.ShapeDtypeStruct((batch * seq, SB_WIDTH), BF16),
        compiler_params=_cparams("parallel", "parallel", "arbitrary"),
        name="sb_attention",
    )(qt, proj, proj, nut)


def _xattn_out_kernel(h_ref, mix_ref, xq_ref, mk_ref, mv_ref, qg_ref, hm_ref, wo_ref, o_ref):
    tm = h_ref.shape[0]
    xq = xq_ref[...].astype(F32)
    ss = _dot((xq * xq).astype(BF16), hm_ref[...])
    qn = (xq * lax.rsqrt(ss * (1.0 / HEAD64) + EPS) * qg_ref[...]).astype(BF16)
    lane = lax.broadcasted_iota(I32, (tm, XA_WIDTH), 1)
    mk = mk_ref[...]
    mv = mv_ref[...]
    xa = jnp.zeros((tm, XA_WIDTH), F32)
    for hh in range(XA_WIDTH // HEAD64):
        in_head = (lane >= hh * HEAD64) & (lane < (hh + 1) * HEAD64)
        s = _dot_nt(jnp.where(in_head, qn, jnp.zeros_like(qn)), mk)
        p = jnp.exp(s - jnp.max(s, axis=1, keepdims=True))
        o = _dot(p.astype(BF16), mv) / jnp.sum(p, axis=1, keepdims=True)
        xa = jnp.where(in_head, o, xa)
    mw = mix_ref.shape[1]
    acc = _dot(mix_ref[...], wo_ref[0:mw, :]) + _dot(xa.astype(BF16), wo_ref[mw:, :])
    o_ref[...] = h_ref[...] + acc


def _xattn_out(h, mix, proj, xq_col_block, memkv, q_gain_row, w_out, seq, mem_len):
    t, d = h.shape
    tm = min(TM_PROJ, seq)
    per_b = seq // tm
    mw = mix.shape[1]
    return pl.pallas_call(
        _xattn_out_kernel,
        grid=(t // tm,),
        in_specs=[
            pl.BlockSpec((tm, d), lambda i: (i, 0)),
            pl.BlockSpec((tm, mw), lambda i: (i, 0)),
            pl.BlockSpec((tm, XA_WIDTH), lambda i: (i, xq_col_block)),
            pl.BlockSpec((mem_len, XA_WIDTH), lambda i: (i // per_b, 0)),
            pl.BlockSpec((mem_len, XA_WIDTH), lambda i: (i // per_b, 1)),
            pl.BlockSpec((1, XA_WIDTH), lambda i: (0, 0)),
            pl.BlockSpec((XA_WIDTH, XA_WIDTH), lambda i: (0, 0)),
            pl.BlockSpec((mw + XA_WIDTH, d), lambda i: (0, 0)),
        ],
        out_specs=pl.BlockSpec((tm, d), lambda i: (i, 0)),
        out_shape=jax.ShapeDtypeStruct((t, d), F32),
        compiler_params=_cparams("parallel"),
        name="xattn_out",
    )(h, mix, proj, memkv, memkv, q_gain_row, _head_sum_matrix(XA_WIDTH, HEAD64), w_out)


def _ffn_kernel(x_ref, g_ref, wg_ref, wu_ref, wd_ref, o_ref, xn_ref, acc_ref):
    f = pl.program_id(1)

    @pl.when(f == 0)
    def _():
        xn_ref[...] = _rms_rows(x_ref[...], g_ref[...]).astype(BF16)
        acc_ref[...] = jnp.zeros_like(acc_ref)

    xn = xn_ref[...]
    g = _dot(xn, wg_ref[...])
    up = _dot(xn, wu_ref[...])
    act = (g * _sigmoid(g) * up).astype(BF16)
    acc_ref[...] += _dot(act, wd_ref[...])

    @pl.when(f == pl.num_programs(1) - 1)
    def _():
        o_ref[...] = x_ref[...] + acc_ref[...]


def _ffn(h, gain, wg, wu, wd):
    t, d = h.shape
    ff = wg.shape[1]
    tm = min(TM_FFN, t)
    tf = TF_FFN
    return pl.pallas_call(
        _ffn_kernel,
        grid=(t // tm, ff // tf),
        in_specs=[
            pl.BlockSpec((tm, d), lambda i, f: (i, 0)),
            pl.BlockSpec((1, d), lambda i, f: (0, 0)),
            pl.BlockSpec((d, tf), lambda i, f: (0, f)),
            pl.BlockSpec((d, tf), lambda i, f: (0, f)),
            pl.BlockSpec((tf, d), lambda i, f: (f, 0)),
        ],
        out_specs=pl.BlockSpec((tm, d), lambda i, f: (i, 0)),
        out_shape=jax.ShapeDtypeStruct((t, d), F32),
        scratch_shapes=[pltpu.VMEM((tm, d), BF16), pltpu.VMEM((tm, d), F32)],
        compiler_params=_cparams("parallel", "arbitrary"),
        name="ffn_dense",
    )(h, gain.reshape(1, d), wg, wu, wd)


def _dn_in_kernel(x_ref, xh_ref, g_ref, w_ref, cw_ref, alog_ref, dtb_ref, tril_ref, ones_ref,
                  qkv_ref, gx_ref, gcb_ref, gl_ref, *, per_b):
    i = pl.program_id(0)
    w3 = qkv_ref.shape[1]
    wgx = gx_ref.shape[1]
    nh = xh_ref.shape[0]
    xn = jnp.concatenate([_rms_rows(xh_ref[...], g_ref[...]).astype(BF16),
                          _rms_rows(x_ref[...], g_ref[...]).astype(BF16)], axis=0)
    proj = _dot(xn, w_ref[...])
    keep = (i % per_b != 0).astype(F32)
    halo = proj[nh - 8:nh, :w3] * keep
    proj = proj[nh:]
    x = proj[:, :w3]
    row8 = lax.broadcasted_iota(I32, (8, w3), 0)
    cw = cw_ref[...]
    acc = x * cw[DN_CONV - 1:DN_CONV, :]
    for s in range(1, DN_CONV):
        xs = pltpu.roll(x, s, 0)
        hs = pltpu.roll(halo, s, 0)
        first = jnp.where(row8 < s, hs, xs[0:8, :])
        xs = jnp.concatenate([first, xs[8:, :]], axis=0)
        acc = acc + xs * cw[DN_CONV - 1 - s:DN_CONV - s, :]
    y = acc * _sigmoid(acc)
    for c in range(3 * DN_HEADS):
        cols = slice(c * DN_HEAD, (c + 1) * DN_HEAD)
        blk = y[:, cols]
        if c < 2 * DN_HEADS:
            blk = blk * lax.rsqrt(jnp.sum(blk * blk, axis=-1, keepdims=True) + EPS)
            if c < DN_HEADS:
                blk = blk * (DN_HEAD ** -0.5)
        qkv_ref[:, cols] = blk.astype(qkv_ref.dtype)
    gx_ref[...] = proj[:, w3:w3 + wgx].astype(gx_ref.dtype)

    ab = proj[:, w3 + wgx:]
    lane = lax.broadcasted_iota(I32, ab.shape, 1)
    g = -jnp.exp(alog_ref[...]) * _softplus(ab + dtb_ref[...])
    g = jnp.where(lane < DN_HEADS, g, 0.0)
    beta = _sigmoid(ab)
    g_hi = g.astype(BF16)
    g_lo = (g - g_hi.astype(F32)).astype(BF16)
    tril = tril_ref[...]
    ones = ones_ref[...]
    gc = _dot(tril, g_hi) + _dot(tril, g_lo)
    gl = _dot(ones, g_hi) + _dot(ones, g_lo)
    gcb_ref[...] = jnp.where(lane < DN_HEADS, gc, beta)
    gl_ref[...] = gl


def _dn_in(h, gain, w, conv_w, alog_row, dtb_row, seq):
    t, d = h.shape
    tm = min(TM_PROJ, seq)
    per_b = seq // tm
    w3 = 3 * DN_WIDTH
    wgx = DN_WIDTH + XA_WIDTH
    hb = 16
    r = jnp.arange(tm)[:, None]
    c = jnp.arange(tm)[None, :]
    same = (r // DN_CHUNK) == (c // DN_CHUNK)
    tril = (same & (c <= r)).astype(BF16)
    ones = same.astype(BF16)
    return pl.pallas_call(
        functools.partial(_dn_in_kernel, per_b=per_b),
        grid=(t // tm,),
        in_specs=[
            pl.BlockSpec((tm, d), lambda i: (i, 0)),
            pl.BlockSpec((hb, d), lambda i: (jnp.maximum(i * (tm // hb) - 1, 0), 0)),
            pl.BlockSpec((1, d), lambda i: (0, 0)),
            pl.BlockSpec(w.shape, lambda i: (0, 0)),
            pl.BlockSpec((DN_CONV, w3), lambda i: (0, 0)),
            pl.BlockSpec((1, LANES), lambda i: (0, 0)),
            pl.BlockSpec((1, LANES), lambda i: (0, 0)),
            pl.BlockSpec((tm, tm), lambda i: (0, 0)),
            pl.BlockSpec((tm, tm), lambda i: (0, 0)),
        ],
        out_specs=[
            pl.BlockSpec((tm, w3), lambda i: (i, 0)),
            pl.BlockSpec((tm, wgx), lambda i: (i, 0)),
            pl.BlockSpec((tm, LANES), lambda i: (i, 0)),
            pl.BlockSpec((tm, LANES), lambda i: (i, 0)),
        ],
        out_shape=[
            jax.ShapeDtypeStruct((t, w3), BF16),
            jax.ShapeDtypeStruct((t, wgx), BF16),
            jax.ShapeDtypeStruct((t, LANES), F32),
            jax.ShapeDtypeStruct((t, LANES), F32),
        ],
        compiler_params=_cparams("parallel"),
        name="dn_in",
    )(h, h, gain.reshape(1, d), w, conv_w, alog_row, dtb_row, tril, ones)


def _delta_kernel(q_ref, k_ref, v_ref, gate_ref, gcb_ref, gl_ref, gct_ref, og_ref, o_ref, s_ref, *, heads):
    h0 = pl.program_id(1) * heads
    rows = q_ref.shape[0]
    n_chunks = rows // DN_CHUNK
    hs = range(heads)

    @pl.when(pl.program_id(2) == 0)
    def _():
        s_ref[...] = jnp.zeros_like(s_ref)

    lane = lax.broadcasted_iota(I32, (rows, LANES), 1)
    gcb = gcb_ref[...]
    glb = gl_ref[...]
    gc_col = [jnp.sum(jnp.where(lane == h0 + h, gcb, 0.0), axis=1, keepdims=True) for h in hs]
    beta_col = [jnp.sum(jnp.where(lane == h0 + h + DN_HEADS, gcb, 0.0), axis=1, keepdims=True) for h in hs]
    gl_col = [jnp.sum(jnp.where(lane == h0 + h, glb, 0.0), axis=1, keepdims=True) for h in hs]

    ri = lax.broadcasted_iota(I32, (rows, rows), 0)
    ci = lax.broadcasted_iota(I32, (rows, rows), 1)
    same = (ri // DN_CHUNK) == (ci // DN_CHUNK)
    tri = same & (ci <= ri)
    strict = same & (ci < ri)
    eye = (ri == ci).astype(F32)
    cols = [slice(h * DN_HEAD, (h + 1) * DN_HEAD) for h in hs]

    decay = [jnp.exp(jnp.where(tri, gc_col[h] - gct_ref[h], -1e30)) for h in hs]
    k = [k_ref[:, cols[h]] for h in hs]
    q = [q_ref[:, cols[h]] for h in hs]
    kf = [k[h].astype(F32) for h in hs]
    eg = [jnp.exp(gc_col[h]) for h in hs]
    kb = [kf[h] * beta_col[h] for h in hs]
    rhs = [jnp.concatenate([v_ref[:, cols[h]].astype(F32) * beta_col[h], kb[h] * eg[h]], axis=1).astype(BF16) for h in hs]
    npow = [jnp.where(strict, -(_dot_nt(kb[h].astype(BF16), k[h]) * decay[h]), 0.0) for h in hs]
    inv = [eye + npow[h] for h in hs]
    for _ in range(5):
        nb = [npow[h].astype(BF16) for h in hs]
        npow = [_dot(nb[h], nb[h]) for h in hs]
        inv = [inv[h] + _dot(npow[h].astype(BF16), inv[h].astype(BF16)) for h in hs]
    uw = [_dot(inv[h].astype(BF16), rhs[h]) for h in hs]
    qk = [(_dot_nt(q[h], k[h]) * decay[h]).astype(BF16) for h in hs]
    w_b = [uw[h][:, DN_HEAD:].astype(BF16) for h in hs]
    q_dec = [(q[h].astype(F32) * eg[h]).astype(BF16) for h in hs]
    k_dec = [(kf[h] * jnp.exp(gl_col[h] - gc_col[h])).astype(BF16) for h in hs]

    state = [s_ref[h] for h in hs]
    outs = [[] for _ in hs]
    for c in range(n_chunks):
        rs = slice(c * DN_CHUNK, (c + 1) * DN_CHUNK)
        sb = [state[h].astype(BF16) for h in hs]
        ws = [_dot(jnp.concatenate([w_b[h][rs], q_dec[h][rs]], axis=0), sb[h]) for h in hs]
        v_new = [(uw[h][rs, :DN_HEAD] - ws[h][:DN_CHUNK]).astype(BF16) for h in hs]
        for h in hs:
            outs[h].append(ws[h][DN_CHUNK:] + _dot(qk[h][rs, rs], v_new[h]))
        state = [state[h] * jnp.exp(gl_col[h][c * DN_CHUNK:c * DN_CHUNK + 1, :]) + _dot_tn(k_dec[h][rs], v_new[h]) for h in hs]
    for h in hs:
        s_ref[h] = state[h]
        o = jnp.concatenate(outs[h], axis=0)
        gate = gate_ref[:, cols[h]].astype(F32)
        o_ref[:, cols[h]] = (_rms_rows(o, og_ref[...]) * (gate * _sigmoid(gate))).astype(o_ref.dtype)


def _delta_rule(qkvn, gx, gcb, gl, gct, o_gain_row, batch, seq):
    rows = min(DN_ROWS, seq)
    steps = seq // rows
    t = batch * seq
    heads = DN_HEADS_PER_STEP
    groups = DN_HEADS // heads
    w = heads * DN_HEAD
    return pl.pallas_call(
        functools.partial(_delta_kernel, heads=heads),
        grid=(batch, groups, steps),
        in_specs=[
            pl.BlockSpec((rows, w), lambda b, g, s: (b * steps + s, g)),
            pl.BlockSpec((rows, w), lambda b, g, s: (b * steps + s, groups + g)),
            pl.BlockSpec((rows, w), lambda b, g, s: (b * steps + s, 2 * groups + g)),
            pl.BlockSpec((rows, w), lambda b, g, s: (b * steps + s, g)),
            pl.BlockSpec((rows, LANES), lambda b, g, s: (b * steps + s, 0)),
            pl.BlockSpec((rows, LANES), lambda b, g, s: (b * steps + s, 0)),
            pl.BlockSpec((heads, 1, rows), lambda b, g, s: (g, 0, b * steps + s)),
            pl.BlockSpec((1, DN_HEAD), lambda b, g, s: (0, 0)),
        ],
        out_specs=pl.BlockSpec((rows, w), lambda b, g, s: (b * steps + s, g)),
        out_shape=jax.ShapeDtypeStruct((t, DN_WIDTH), BF16),
        scratch_shapes=[pltpu.VMEM((heads, DN_HEAD, DN_HEAD), F32)],
        compiler_params=_cparams("parallel", "parallel", "arbitrary"),
        name="delta_rule",
    )(qkvn, qkvn, qkvn, gx, gcb, gl, gct, o_gain_row)


def _router_kernel(h_ref, g_ref, rt_ref, us_ref, hn_ref, pos_ref, wts_ref, cstart_ref, carry_ref):
    i = pl.program_id(0)

    @pl.when(i == 0)
    def _():
        carry_ref[...] = jnp.zeros_like(carry_ref)

    hn = _rms_rows(h_ref[...], g_ref[...])
    hn_ref[...] = hn.astype(BF16)
    logits = lax.dot_general(rt_ref[...], hn, (((1,), (1,)), ((), ())),
                             precision=lax.Precision.HIGHEST, preferred_element_type=F32)
    sub = lax.broadcasted_iota(I32, logits.shape, 0)
    m1 = jnp.max(logits, axis=0, keepdims=True)
    i1 = jnp.min(jnp.where(logits == m1, sub, N_EXPERTS), axis=0, keepdims=True)
    rest = jnp.where(sub == i1, -jnp.inf, logits)
    m2 = jnp.max(rest, axis=0, keepdims=True)
    i2 = jnp.min(jnp.where(rest == m2, sub, N_EXPERTS), axis=0, keepdims=True)
    e21 = jnp.exp(m2 - m1)
    w1 = 1.0 / (1.0 + e21)
    w2 = e21 / (1.0 + e21)
    hit1 = sub == i1
    hit2 = sub == i2
    chosen = hit1 | hit2
    onehot = jnp.where(chosen, 1.0, 0.0)
    carry = carry_ref[...]
    cstart_ref[0] = carry
    before = _dot(onehot.astype(BF16), us_ref[...]) + carry[:, 0:1]
    carry_ref[...] = carry + jnp.sum(onehot, axis=1, keepdims=True)
    pos_ref[...] = jnp.where(chosen, before, -1.0)
    wts_ref[...] = jnp.where(hit1, w1, jnp.where(hit2, w2, 0.0))


def _router(h, gain, router_t):
    t, d = h.shape
    tm = min(MOE_BLK, t)
    nwin = t // tm
    r = jnp.arange(tm)[:, None]
    c = jnp.arange(tm)[None, :]
    us = (r < c).astype(BF16)
    return pl.pallas_call(
        _router_kernel,
        grid=(nwin,),
        in_specs=[
            pl.BlockSpec((tm, d), lambda i: (i, 0)),
            pl.BlockSpec((1, d), lambda i: (0, 0)),
            pl.BlockSpec((N_EXPERTS, d), lambda i: (0, 0)),
            pl.BlockSpec((tm, tm), lambda i: (0, 0)),
        ],
        out_specs=[
            pl.BlockSpec((tm, d), lambda i: (i, 0)),
            pl.BlockSpec((N_EXPERTS, tm), lambda i: (0, i)),
            pl.BlockSpec((N_EXPERTS, tm), lambda i: (0, i)),
            pl.BlockSpec((1, N_EXPERTS, LANES), lambda i: (i, 0, 0)),
        ],
        out_shape=[
            jax.ShapeDtypeStruct((t, d), BF16),
            jax.ShapeDtypeStruct((N_EXPERTS, t), F32),
            jax.ShapeDtypeStruct((N_EXPERTS, t), F32),
            jax.ShapeDtypeStruct((nwin, N_EXPERTS, LANES), F32),
        ],
        scratch_shapes=[pltpu.VMEM((N_EXPERTS, LANES), F32)],
        compiler_params=_cparams("arbitrary"),
        name="moe_router",
    )(h, gain.reshape(1, d), router_t, us)


def _dispatch_kernel(blk_ref, win_ref, flag_ref, exp_ref, e0_ref, e1_ref, e2_ref, hn_ref, slots_ref, wts_ref, xs_ref, ws_ref):
    l = pl.program_id(0)
    flag = flag_ref[l]
    nslot = xs_ref.shape[0]
    sub = hn_ref.shape[0] // 2

    @pl.when((flag & 2) != 0)
    def _():
        xs_ref[...] = jnp.zeros_like(xs_ref)
        ws_ref[...] = jnp.zeros_like(ws_ref)

    edges = (e0_ref[l], e1_ref[l], e2_ref[l])
    for s in range(2):
        for part in range(nslot // MOE_PART):
            r0 = part * MOE_PART

            @pl.when(((flag & 1) != 0) & (edges[s] < r0 + MOE_PART) & (edges[s + 1] > r0) & (edges[s + 1] > edges[s]))
            def _():
                e = exp_ref[l]
                base = (blk_ref[l] * nslot + r0).astype(F32)
                toks = slice(s * sub, (s + 1) * sub)
                rel = slots_ref[pl.ds(e, 1), toks] - base
                hit = rel == lax.broadcasted_iota(I32, (MOE_PART, sub), 0).astype(F32)
                rows = pl.ds(r0, MOE_PART)
                xs_ref[rows, :] += _dot(jnp.where(hit, 1.0, 0.0).astype(BF16), hn_ref[toks, :]).astype(xs_ref.dtype)
                ws_ref[rows, :] += jnp.sum(jnp.where(hit, wts_ref[pl.ds(e, 1), toks], 0.0), axis=1, keepdims=True)


def _dispatch(hn, slots_rows, wts, pairs, cap):
    t, d = hn.shape
    b = MOE_BLK
    w = 2 * b
    return pl.pallas_call(
        _dispatch_kernel,
        grid_spec=pltpu.PrefetchScalarGridSpec(
            num_scalar_prefetch=len(pairs),
            grid=(pairs[0].shape[0],),
            in_specs=[
                pl.BlockSpec((w, d), lambda l, blk, win, *_: (win[l], 0)),
                pl.BlockSpec((N_EXPERTS, w), lambda l, blk, win, *_: (0, win[l])),
                pl.BlockSpec((N_EXPERTS, w), lambda l, blk, win, *_: (0, win[l])),
            ],
            out_specs=[
                pl.BlockSpec((b, d), lambda l, blk, win, *_: (blk[l], 0)),
                pl.BlockSpec((b, LANES), lambda l, blk, win, *_: (blk[l], 0)),
            ],
        ),
        out_shape=[jax.ShapeDtypeStruct((cap, d), BF16), jax.ShapeDtypeStruct((cap, LANES), F32)],
        compiler_params=_cparams("arbitrary"),
        name="moe_dispatch",
    )(*pairs, hn, slots_rows, wts)


def _experts_kernel(be_ref, src_ref, used_ref, xs_ref, ws_ref, wg_ref, wu_ref, wd_ref, y_ref, acc_ref):
    b = pl.program_id(0)
    f = pl.program_id(1)
    used = used_ref[b] != 0

    @pl.when(f == 0)
    def _():
        acc_ref[...] = jnp.zeros_like(acc_ref)

    @pl.when(used)
    def _():
        xs = xs_ref[...]
        g = _dot(xs, wg_ref[...])
        up = _dot(xs, wu_ref[...])
        act = (g * _sigmoid(g) * up).astype(BF16)
        acc_ref[...] += _dot(act, wd_ref[...])

    @pl.when(f == pl.num_programs(1) - 1)
    def _():
        y_ref[...] = jnp.where(used, acc_ref[...] * ws_ref[:, 0:1], 0.0).astype(y_ref.dtype)


def _experts(xs, ws, wg, wu, wd, blk_expert, blk_src, blk_used):
    cap, d = xs.shape
    ff = wg.shape[2]
    b = MOE_BLK
    tf = TF_MOE
    return pl.pallas_call(
        _experts_kernel,
        grid_spec=pltpu.PrefetchScalarGridSpec(
            num_scalar_prefetch=3,
            grid=(cap // b, ff // tf),
            in_specs=[
                pl.BlockSpec((b, d), lambda i, f, be, src, used: (src[i], 0)),
                pl.BlockSpec((b, LANES), lambda i, f, be, src, used: (src[i], 0)),
                pl.BlockSpec((None, d, tf), lambda i, f, be, src, used: (be[i], 0, f)),
                pl.BlockSpec((None, d, tf), lambda i, f, be, src, used: (be[i], 0, f)),
                pl.BlockSpec((None, tf, d), lambda i, f, be, src, used: (be[i], f, 0)),
            ],
            out_specs=pl.BlockSpec((b, d), lambda i, f, be, src, used: (i, 0)),
            scratch_shapes=[pltpu.VMEM((b, d), F32)],
        ),
        out_shape=jax.ShapeDtypeStruct((cap, d), BF16),
        compiler_params=_cparams("parallel", "arbitrary"),
        name="moe_experts",
    )(blk_expert, blk_src, blk_used, xs, ws, wg, wu, wd)


def _combine_kernel(blk_ref, win_ref, flag_ref, exp_ref, e0_ref, e1_ref, e2_ref, h_ref, y_ref, slots_ref, o_ref):
    l = pl.program_id(0)
    flag = flag_ref[l]

    @pl.when((flag & 2) != 0)
    def _():
        o_ref[...] = h_ref[...]

    nslot = y_ref.shape[0]
    sub = h_ref.shape[0] // 2
    edges = (e0_ref[l], e1_ref[l], e2_ref[l])
    for s in range(2):
        for part in range(nslot // MOE_PART):
            c0 = part * MOE_PART

            @pl.when(((flag & 1) != 0) & (edges[s] < c0 + MOE_PART) & (edges[s + 1] > c0) & (edges[s + 1] > edges[s]))
            def _():
                toks = pl.ds(s * sub, sub)
                cols = slots_ref[toks, :]
                mine = lax.broadcasted_iota(I32, cols.shape, 1) == exp_ref[l]
                base = (blk_ref[l] * nslot + c0).astype(F32)
                rel = jnp.sum(jnp.where(mine, cols, 0.0), axis=1, keepdims=True) - base
                hit = rel == lax.broadcasted_iota(I32, (sub, MOE_PART), 1).astype(F32)
                o_ref[toks, :] += _dot(jnp.where(hit, 1.0, 0.0).astype(BF16), y_ref[pl.ds(c0, MOE_PART), :])


def _combine(h, y, slots_cols, pairs):
    t, d = h.shape
    b = MOE_BLK
    w = 2 * b
    return pl.pallas_call(
        _combine_kernel,
        grid_spec=pltpu.PrefetchScalarGridSpec(
            num_scalar_prefetch=len(pairs),
            grid=(pairs[0].shape[0],),
            in_specs=[
                pl.BlockSpec((w, d), lambda l, blk, win, *_: (win[l], 0)),
                pl.BlockSpec((b, d), lambda l, blk, win, *_: (blk[l], 0)),
                pl.BlockSpec((w, N_EXPERTS), lambda l, blk, win, *_: (win[l], 0)),
            ],
            out_specs=pl.BlockSpec((w, d), lambda l, blk, win, *_: (win[l], 0)),
        ),
        out_shape=jax.ShapeDtypeStruct((t, d), F32),
        compiler_params=_cparams("arbitrary"),
        name="moe_combine",
    )(*pairs, h, y, slots_cols)


def _pair_list(edges, expert_major, n_pairs):
    b = MOE_BLK
    n_e, n_w = edges[0].shape
    first_blk = edges[0] // b
    n_blk = jnp.where(edges[2] > edges[0], (edges[2] - 1) // b - first_blk + 1, 0)
    order = (lambda a: a.reshape(-1)) if expert_major else (lambda a: a.T.reshape(-1))
    win_of = order(jnp.broadcast_to(jnp.arange(n_w, dtype=I32)[None, :], (n_e, n_w)))
    exp_of = order(jnp.broadcast_to(jnp.arange(n_e, dtype=I32)[:, None], (n_e, n_w)))
    fb, nb = order(first_blk), order(n_blk)
    end = jnp.cumsum(nb)
    total = end[-1]
    l = jnp.arange(n_pairs, dtype=I32)
    lc = jnp.minimum(l, total - 1)
    p = jnp.sum((end[None, :] <= lc[:, None]).astype(I32), axis=1)
    onehot = jnp.arange(fb.shape[0], dtype=I32)[None, :] == p[:, None]

    def pick(v):
        return jnp.sum(jnp.where(onehot, v[None, :], 0), axis=1)

    blk = pick(fb) + (lc - (pick(end) - pick(nb)))
    win = pick(win_of)
    major = blk if expert_major else win
    first = jnp.concatenate([jnp.ones((1,), bool), major[1:] != major[:-1]])
    valid = l < total
    flag = valid.astype(I32) + 2 * (first & valid).astype(I32)
    rel = [jnp.clip(pick(order(e)) - blk * b, 0, b) for e in edges]
    return (blk, win, flag, pick(exp_of), *rel)


def _moe(h, gain, router, wg, wu, wd):
    t, d = h.shape
    b = MOE_BLK
    nwin = t // b
    nw = nwin // 2
    nblk = (2 * t) // b + N_EXPERTS
    cap = nblk * b
    n_pairs = nblk + N_EXPERTS * nw

    hn, pos, wts, cstart = _router(h, gain, router.T.astype(F32))

    cstart = cstart[:, :, 0].astype(I32).T
    routed = pos >= 0.0
    counts = jnp.sum(routed.astype(I32), axis=1)
    padded = (counts + b - 1) // b * b
    gend = jnp.cumsum(padded)
    gstart = gend - padded
    slots_rows = jnp.where(routed, pos + gstart[:, None].astype(F32), -1.0)
    slots_cols = slots_rows.T

    bounds = gstart[:, None] + jnp.concatenate([cstart, counts[:, None]], axis=1)
    edges = [bounds[:, k:k + 2 * nw:2] for k in range(3)]
    pairs_d = _pair_list(edges, True, n_pairs)
    pairs_c = _pair_list(edges, False, n_pairs)

    n_used = gend[-1] // b
    bidx = jnp.arange(nblk, dtype=I32)
    blk_used = (bidx < n_used).astype(I32)
    blk_src = jnp.minimum(bidx, n_used - 1)
    blk_expert = jnp.minimum(jnp.sum((gend[None, :] <= (blk_src * b)[:, None]).astype(I32), axis=1), N_EXPERTS - 1)

    xs, ws = _dispatch(hn, slots_rows, wts, pairs_d, cap)
    y = _experts(xs, ws, wg, wu, wd, blk_expert, blk_src, blk_used)
    return _combine(h, y, slots_cols, pairs_c)


def _tile_gain(gain, reps, scale=1.0):
    return (jnp.tile(gain.astype(F32), reps) * scale).reshape(1, -1)


def _mem_kv(mem2d, gain, w, k_gain):
    return _norm_proj(mem2d, gain, w.astype(BF16), _tile_gain(k_gain, XA_WIDTH // HEAD64), XA_WIDTH)[0]


def kernel(x, mem, mix_norm, ffn_norm, mem_norm, w_mem_kv, xa_q_norm, xa_k_norm, w_out, sb_w_in, sb_q_norm, sb_k_norm, dn_w_in, dn_conv, dn_a_log, dn_dt_bias, dn_o_norm, ffd_w_gate, ffd_w_up, ffd_w_down, moe_router, moe_w_gate, moe_w_up, moe_w_down):
    batch, seq, d = x.shape
    mem_len = mem.shape[1]
    t = batch * seq
    h = x.reshape(t, d)
    mem2d = mem.reshape(batch * mem_len, d)
    scale64 = HEAD64 ** -0.5

    qk_gain = jnp.concatenate([_tile_gain(sb_q_norm[0], SB_HEADS, scale64 * LOG2E), _tile_gain(sb_k_norm[0], SB_HEADS)], axis=1)
    proj0, qt0 = _norm_proj(h, mix_norm[0], sb_w_in[0].astype(BF16), qk_gain, 2 * SB_WIDTH, SB_WIDTH)
    mix0 = _sb_attention(proj0, qt0, batch, seq)
    memkv0 = _mem_kv(mem2d, mem_norm[0], w_mem_kv[0], xa_k_norm[0])
    xq_gain0 = _tile_gain(xa_q_norm[0], XA_WIDTH // HEAD64, scale64)
    h = _xattn_out(h, mix0, proj0, (3 * SB_WIDTH) // XA_WIDTH, memkv0, xq_gain0, w_out[0].astype(BF16), seq, mem_len)
    h = _ffn(h, ffn_norm[0], ffd_w_gate[0].astype(BF16), ffd_w_up[0].astype(BF16), ffd_w_down[0].astype(BF16))

    w1 = dn_w_in[0]
    w1 = jnp.concatenate([w1[:, :4 * DN_WIDTH], w1[:, 4 * DN_WIDTH + 2 * DN_HEADS:], w1[:, 4 * DN_WIDTH:4 * DN_WIDTH + 2 * DN_HEADS],
                          jnp.zeros((d, LANES - 2 * DN_HEADS), F32)], axis=1).astype(BF16)
    pad = LANES - DN_HEADS
    alog_row = jnp.pad(dn_a_log[0].astype(F32), (0, pad)).reshape(1, LANES)
    dtb_row = jnp.pad(dn_dt_bias[0].astype(F32), (0, pad)).reshape(1, LANES)
    qkvn, gx, gcb, gl = _dn_in(h, mix_norm[1], w1, dn_conv[0].astype(F32), alog_row, dtb_row, seq)
    gct = gcb[:, :DN_HEADS].T.reshape(DN_HEADS, 1, t)
    mix1 = _delta_rule(qkvn, gx, gcb, gl, gct, dn_o_norm[0].astype(F32).reshape(1, DN_HEAD), batch, seq)
    memkv1 = _mem_kv(mem2d, mem_norm[1], w_mem_kv[1], xa_k_norm[1])
    xq_gain1 = _tile_gain(xa_q_norm[1], XA_WIDTH // HEAD64, scale64)
    h = _xattn_out(h, mix1, gx, DN_WIDTH // XA_WIDTH, memkv1, xq_gain1, w_out[1].astype(BF16), seq, mem_len)
    h = _moe(h, ffn_norm[1], moe_router[0], moe_w_gate[0].astype(BF16), moe_w_up[0].astype(BF16), moe_w_down[0].astype(BF16))
    return h.reshape(batch, seq, d)
```

```python
import functools

import jax
import jax.numpy as jnp
from jax import lax
from jax.experimental import pallas as pl
from jax.experimental.pallas import tpu as pltpu

F32 = jnp.float32
BF16 = jnp.bfloat16
I32 = jnp.int32

EPS = 1e-6
LOG2E = 1.4426950408889634
D_MODEL = 1024
HEAD64 = 64
SB_HEADS = 12
SB_WIDTH = SB_HEADS * HEAD64
XA_WIDTH = 256
DN_HEADS = 6
DN_HEAD = 128
DN_WIDTH = DN_HEADS * DN_HEAD
DN_CONV = 4
DN_CHUNK = 64
D_FF = 3584
N_EXPERTS = 8

LANES = 128
MXU_DIM = 256
VMEM_LIMIT = 56 * 1024 * 1024

TM_PROJ = 512
TM_FFN = 1024
TF_FFN = 512
TF_MOE = 1792
SB_TQ = 512
SB_TK = 256
SB_ZERO_LOG2 = -160.0
DN_ROWS = 256
DN_HEADS_PER_STEP = 6
MOE_BLK = 512
MOE_PART = 256
MOE_SUBWINS = 4


def _cparams(*sem):
    return pltpu.CompilerParams(dimension_semantics=sem, vmem_limit_bytes=VMEM_LIMIT)


def _dot(a, b):
    return jnp.dot(a, b, preferred_element_type=F32)


def _dot_nt(a, b):
    return lax.dot_general(a, b, (((1,), (1,)), ((), ())), preferred_element_type=F32)


def _dot_tn(a, b):
    return lax.dot_general(a, b, (((0,), (0,)), ((), ())), preferred_element_type=F32)


def _rms_rows(x, gain_row):
    ms = jnp.mean(x * x, axis=-1, keepdims=True)
    return x * lax.rsqrt(ms + EPS) * gain_row


def _softplus(x):
    return jnp.maximum(x, 0.0) + jnp.log1p(jnp.exp(-jnp.abs(x)))


def _sigmoid(x):
    return 1.0 / (1.0 + jnp.exp(-x))


def _head_sum_matrix(width, head):
    r = jnp.arange(width)[:, None] // head
    c = jnp.arange(width)[None, :] // head
    return (r == c).astype(BF16)


def _norm_proj_kernel(x_ref, g_ref, w_ref, hg_ref, hm_ref, o_ref, *t_refs, n_norm_cols):
    xn = _rms_rows(x_ref[...], g_ref[...]).astype(BF16)
    proj = _dot(xn, w_ref[...])
    n_t_cols = t_refs[0].shape[0] if t_refs else 0
    for c in range(n_norm_cols // MXU_DIM):
        cols = slice(c * MXU_DIM, (c + 1) * MXU_DIM)
        blk = proj[:, cols]
        ss = _dot((blk * blk).astype(BF16), hm_ref[...])
        blk = blk * lax.rsqrt(ss * (1.0 / HEAD64) + EPS) * hg_ref[:, cols]
        o_ref[:, cols] = blk.astype(o_ref.dtype)
        if c * MXU_DIM < n_t_cols:
            t_refs[0][cols, :] = blk.T.astype(t_refs[0].dtype)
    o_ref[:, n_norm_cols:] = proj[:, n_norm_cols:].astype(o_ref.dtype)


def _norm_proj(x, gain, w, head_gain, n_norm_cols, n_t_cols=0):
    t, d = x.shape
    n = w.shape[1]
    tm = min(TM_PROJ, t)
    out_shape = [jax.ShapeDtypeStruct((t, n), BF16)]
    out_specs = [pl.BlockSpec((tm, n), lambda i: (i, 0))]
    if n_t_cols:
        out_shape.append(jax.ShapeDtypeStruct((n_t_cols, t), BF16))
        out_specs.append(pl.BlockSpec((n_t_cols, tm), lambda i: (0, i)))
    hg = head_gain
    res = pl.pallas_call(
        functools.partial(_norm_proj_kernel, n_norm_cols=n_norm_cols),
        grid=(t // tm,),
        in_specs=[
            pl.BlockSpec((tm, d), lambda i: (i, 0)),
            pl.BlockSpec((1, d), lambda i: (0, 0)),
            pl.BlockSpec((d, n), lambda i: (0, 0)),
            pl.BlockSpec(hg.shape, lambda i: (0, 0)),
            pl.BlockSpec((MXU_DIM, MXU_DIM), lambda i: (0, 0)),
        ],
        out_specs=out_specs,
        out_shape=out_shape,
        compiler_params=_cparams("parallel"),
        name="norm_proj",
    )(x, gain.reshape(1, d), w, hg, _head_sum_matrix(MXU_DIM, HEAD64))
    return res


def _sb_attn_kernel(qt_ref, k_ref, v_ref, nut_ref, o_ref):
    i = pl.program_id(2)
    tq = qt_ref.shape[1]
    tk = nut_ref.shape[0]
    qt = qt_ref[...]
    sub = lax.broadcasted_iota(I32, qt.shape, 0)
    zero = jnp.zeros_like(qt)
    qqt = jnp.concatenate([jnp.where(sub < HEAD64, qt, zero), jnp.where(sub >= HEAD64, qt, zero)], axis=1)
    nut = nut_ref[...]

    def scores(off, qq):
        return _dot(k_ref[pl.ds(off, tk), :], qq)

    def logits(z, off, qpos):
        sp = jnp.maximum(z, jnp.log2(1.0 + jnp.exp2(jnp.minimum(z, 126.0))))
        log_b = z - sp
        causal = None
        if qpos is not None:
            causal = off + lax.broadcasted_iota(I32, z.shape, 0) < qpos
            sp = jnp.where(causal, sp, 0.0)
        spb = sp.astype(BF16)
        return log_b, spb, causal, _dot(nut, spb)

    def values(off, log_b, spb, causal, within, carry):
        arg = log_b + within + carry
        if causal is not None:
            arg = jnp.where(causal, arg, -1e30)
        pv = _dot_tn(v_ref[pl.ds(off, tk), :], jnp.exp2(arg).astype(BF16))
        return pv, carry + within[0:1, :] - spb[0:1, :].astype(F32)

    def tile_pair(p, carry, qq):
        offs = [pl.multiple_of((2 * p + 1) * tk, tk), pl.multiple_of(2 * p * tk, tk)]
        zs = [scores(off, qq) for off in offs]
        parts = [logits(z, off, None) for z, off in zip(zs, offs)]
        pv_hi, carry = values(offs[0], *parts[0], carry)
        pv_lo, carry = values(offs[1], *parts[1], carry)
        return pv_hi + pv_lo, carry

    hq = tq // 2
    early = [slice(0, hq), slice(tq, tq + hq)]
    late = [slice(hq, tq), slice(tq + hq, 2 * tq)]
    last = ((i + 1) * tq - 1) // (2 * tk)
    qi = lax.broadcasted_iota(I32, (1, 2 * tq), 1)
    qpos = i * tq + jnp.where(qi >= tq, qi - tq, qi)

    off_hi = pl.multiple_of((2 * last + 1) * tk, tk)
    off_lo = pl.multiple_of(2 * last * tk, tk)
    z_hi = scores(off_hi, jnp.concatenate([qqt[:, s] for s in late], axis=1))
    z_lo = scores(off_lo, qqt)
    part_hi = logits(z_hi, off_hi, jnp.concatenate([qpos[:, s] for s in late], axis=1))
    part_lo = logits(z_lo, off_lo, qpos)
    pv_hi, c_late = values(off_hi, *part_hi, jnp.zeros((1, tq), F32))
    zeros_h = jnp.zeros((1, hq), F32)
    pv_lo, carry = values(off_lo, *part_lo, jnp.concatenate([zeros_h, c_late[:, :hq], zeros_h, c_late[:, hq:]], axis=1))
    acc = jnp.concatenate([pv_lo[:, early[0]], pv_lo[:, late[0]] + pv_hi[:, :hq],
                           pv_lo[:, early[1]], pv_lo[:, late[1]] + pv_hi[:, hq:]], axis=1)


    def prev_full(acc, carry):
        pv, carry = tile_pair(last - 1, carry, qqt)
        return acc + pv, carry

    def prev_early(acc, carry):
        pv, ce = tile_pair(last - 1, jnp.concatenate([carry[:, s] for s in early], axis=1),
                           jnp.concatenate([qqt[:, s] for s in early], axis=1))
        acc = jnp.concatenate([acc[:, early[0]] + pv[:, :hq], acc[:, late[0]],
                               acc[:, early[1]] + pv[:, hq:], acc[:, late[1]]], axis=1)
        carry = jnp.concatenate([ce[:, :hq], carry[:, late[0]], ce[:, hq:], carry[:, late[1]]], axis=1)
        return acc, carry

    def prev_pair(acc, carry):
        late_live = jnp.max(jnp.concatenate([carry[:, s] for s in late], axis=1)) > SB_ZERO_LOG2
        return lax.cond(late_live, prev_full, prev_early, acc, carry)

    acc, carry = lax.cond(last > 0, prev_pair, lambda a, c: (a, c), acc, carry)
    rest = jnp.maximum(last - 1, 0)

    def live(st):
        return (st[0] < rest) & (jnp.max(st[2]) > SB_ZERO_LOG2)

    def body(st):
        jj, acc, carry = st
        pv, carry = tile_pair(rest - 1 - jj, carry, qqt)
        return jj + 1, acc + pv, carry

    _, acc, _ = lax.while_loop(live, body, (jnp.int32(0), acc, carry))
    lane = lax.broadcasted_iota(I32, (tq, LANES), 1)
    o_ref[...] = jnp.where(lane < HEAD64, acc[:, :tq].T, acc[:, tq:].T).astype(o_ref.dtype)


def _sb_attention(proj, qt, batch, seq):
    tq = min(SB_TQ, seq)
    nq = seq // tq
    n_pairs = SB_WIDTH // LANES
    tk = min(SB_TK, seq // 2)
    assert tq == 2 * tk, "the diagonal step covers the query tile with exactly two key tiles"
    s_row = jnp.arange(tk)[:, None]
    j_col = jnp.arange(tk)[None, :]
    nut = -(j_col > s_row).astype(BF16)
    return pl.pallas_call(
        _sb_attn_kernel,
        grid=(batch, n_pairs, nq),
        in_specs=[
            pl.BlockSpec((LANES, tq), lambda b, p, i: (p, b * nq + i)),
            pl.BlockSpec((seq, LANES), lambda b, p, i: (b, n_pairs + p)),
            pl.BlockSpec((seq, LANES), lambda b, p, i: (b, 2 * n_pairs + p)),
            pl.BlockSpec((tk, tk), lambda b, p, i: (0, 0)),
        ],
        out_specs=pl.BlockSpec((tq, LANES), lambda b, p, i: (b * nq + i, p)),
        out_shape=jax.ShapeDtypeStruct((batch * seq, SB_WIDTH), BF16),
        compiler_params=_cparams("parallel", "parallel", "arbitrary"),
        name="sb_attention",
    )(qt, proj, proj, nut)


def _xattn_out_kernel(h_ref, mix_ref, xq_ref, mk_ref, mv_ref, qg_ref, hm_ref, wo_ref, o_ref):
    tm = h_ref.shape[0]
    xq = xq_ref[...].astype(F32)
    ss = _dot((xq * xq).astype(BF16), hm_ref[...])
    qn = (xq * lax.rsqrt(ss * (1.0 / HEAD64) + EPS) * qg_ref[...]).astype(BF16)
    lane = lax.broadcasted_iota(I32, (tm, XA_WIDTH), 1)
    mk = mk_ref[...]
    mv = mv_ref[...]
    xa = jnp.zeros((tm, XA_WIDTH), F32)
    for hh in range(XA_WIDTH // HEAD64):
        in_head = (lane >= hh * HEAD64) & (lane < (hh + 1) * HEAD64)
        s = _dot_nt(jnp.where(in_head, qn, jnp.zeros_like(qn)), mk)
        p = jnp.exp(s - jnp.max(s, axis=1, keepdims=True))
        o = _dot(p.astype(BF16), mv) / jnp.sum(p, axis=1, keepdims=True)
        xa = jnp.where(in_head, o, xa)
    mw = mix_ref.shape[1]
    acc = _dot(mix_ref[...], wo_ref[0:mw, :]) + _dot(xa.astype(BF16), wo_ref[mw:, :])
    o_ref[...] = h_ref[...] + acc


def _xattn_out(h, mix, proj, xq_col_block, memkv, q_gain_row, w_out, seq, mem_len):
    t, d = h.shape
    tm = min(TM_PROJ, seq)
    per_b = seq // tm
    mw = mix.shape[1]
    return pl.pallas_call(
        _xattn_out_kernel,
        grid=(t // tm,),
        in_specs=[
            pl.BlockSpec((tm, d), lambda i: (i, 0)),
            pl.BlockSpec((tm, mw), lambda i: (i, 0)),
            pl.BlockSpec((tm, XA_WIDTH), lambda i: (i, xq_col_block)),
            pl.BlockSpec((mem_len, XA_WIDTH), lambda i: (i // per_b, 0)),
            pl.BlockSpec((mem_len, XA_WIDTH), lambda i: (i // per_b, 1)),
            pl.BlockSpec((1, XA_WIDTH), lambda i: (0, 0)),
            pl.BlockSpec((XA_WIDTH, XA_WIDTH), lambda i: (0, 0)),
            pl.BlockSpec((mw + XA_WIDTH, d), lambda i: (0, 0)),
        ],
        out_specs=pl.BlockSpec((tm, d), lambda i: (i, 0)),
        out_shape=jax.ShapeDtypeStruct((t, d), F32),
        compiler_params=_cparams("parallel"),
        name="xattn_out",
    )(h, mix, proj, memkv, memkv, q_gain_row, _head_sum_matrix(XA_WIDTH, HEAD64), w_out)


def _ffn_kernel(x_ref, g_ref, wg_ref, wu_ref, wd_ref, o_ref, xn_ref, acc_ref):
    f = pl.program_id(1)

    @pl.when(f == 0)
    def _():
        xn_ref[...] = _rms_rows(x_ref[...], g_ref[...]).astype(BF16)
        acc_ref[...] = jnp.zeros_like(acc_ref)

    xn = xn_ref[...]
    g = _dot(xn, wg_ref[...])
    up = _dot(xn, wu_ref[...])
    act = (g * _sigmoid(g) * up).astype(BF16)
    acc_ref[...] += _dot(act, wd_ref[...])

    @pl.when(f == pl.num_programs(1) - 1)
    def _():
        o_ref[...] = x_ref[...] + acc_ref[...]


def _ffn(h, gain, wg, wu, wd):
    t, d = h.shape
    ff = wg.shape[1]
    tm = min(TM_FFN, t)
    tf = TF_FFN
    return pl.pallas_call(
        _ffn_kernel,
        grid=(t // tm, ff // tf),
        in_specs=[
            pl.BlockSpec((tm, d), lambda i, f: (i, 0)),
            pl.BlockSpec((1, d), lambda i, f: (0, 0)),
            pl.BlockSpec((d, tf), lambda i, f: (0, f)),
            pl.BlockSpec((d, tf), lambda i, f: (0, f)),
            pl.BlockSpec((tf, d), lambda i, f: (f, 0)),
        ],
        out_specs=pl.BlockSpec((tm, d), lambda i, f: (i, 0)),
        out_shape=jax.ShapeDtypeStruct((t, d), F32),
        scratch_shapes=[pltpu.VMEM((tm, d), BF16), pltpu.VMEM((tm, d), F32)],
        compiler_params=_cparams("parallel", "arbitrary"),
        name="ffn_dense",
    )(h, gain.reshape(1, d), wg, wu, wd)


def _dn_in_kernel(x_ref, xh_ref, g_ref, w_ref, cw_ref, alog_ref, dtb_ref, tril_ref, ones_ref,
                  qkv_ref, gx_ref, gcb_ref, gl_ref, *, per_b):
    i = pl.program_id(0)
    w3 = qkv_ref.shape[1]
    wgx = gx_ref.shape[1]
    nh = xh_ref.shape[0]
    xn = jnp.concatenate([_rms_rows(xh_ref[...], g_ref[...]).astype(BF16),
                          _rms_rows(x_ref[...], g_ref[...]).astype(BF16)], axis=0)
    proj = _dot(xn, w_ref[...])
    keep = (i % per_b != 0).astype(F32)
    halo = proj[nh - 8:nh, :w3] * keep
    proj = proj[nh:]
    x = proj[:, :w3]
    row8 = lax.broadcasted_iota(I32, (8, w3), 0)
    cw = cw_ref[...]
    acc = x * cw[DN_CONV - 1:DN_CONV, :]
    for s in range(1, DN_CONV):
        xs = pltpu.roll(x, s, 0)
        hs = pltpu.roll(halo, s, 0)
        first = jnp.where(row8 < s, hs, xs[0:8, :])
        xs = jnp.concatenate([first, xs[8:, :]], axis=0)
        acc = acc + xs * cw[DN_CONV - 1 - s:DN_CONV - s, :]
    y = acc * _sigmoid(acc)
    for c in range(3 * DN_HEADS):
        cols = slice(c * DN_HEAD, (c + 1) * DN_HEAD)
        blk = y[:, cols]
        if c < 2 * DN_HEADS:
            blk = blk * lax.rsqrt(jnp.sum(blk * blk, axis=-1, keepdims=True) + EPS)
            if c < DN_HEADS:
                blk = blk * (DN_HEAD ** -0.5)
        qkv_ref[:, cols] = blk.astype(qkv_ref.dtype)
    gx_ref[...] = proj[:, w3:w3 + wgx].astype(gx_ref.dtype)

    ab = proj[:, w3 + wgx:]
    lane = lax.broadcasted_iota(I32, ab.shape, 1)
    g = -jnp.exp(alog_ref[...]) * _softplus(ab + dtb_ref[...])
    g = jnp.where(lane < DN_HEADS, g, 0.0)
    beta = _sigmoid(ab)
    g_hi = g.astype(BF16)
    g_lo = (g - g_hi.astype(F32)).astype(BF16)
    tril = tril_ref[...]
    ones = ones_ref[...]
    gc = _dot(tril, g_hi) + _dot(tril, g_lo)
    gl = _dot(ones, g_hi) + _dot(ones, g_lo)
    gcb_ref[...] = jnp.where(lane < DN_HEADS, gc, beta)
    gl_ref[...] = gl


def _dn_in(h, gain, w, conv_w, alog_row, dtb_row, seq):
    t, d = h.shape
    tm = min(TM_PROJ, seq)
    per_b = seq // tm
    w3 = 3 * DN_WIDTH
    wgx = DN_WIDTH + XA_WIDTH
    hb = 16
    r = jnp.arange(tm)[:, None]
    c = jnp.arange(tm)[None, :]
    same = (r // DN_CHUNK) == (c // DN_CHUNK)
    tril = (same & (c <= r)).astype(BF16)
    ones = same.astype(BF16)
    return pl.pallas_call(
        functools.partial(_dn_in_kernel, per_b=per_b),
        grid=(t // tm,),
        in_specs=[
            pl.BlockSpec((tm, d), lambda i: (i, 0)),
            pl.BlockSpec((hb, d), lambda i: (jnp.maximum(i * (tm // hb) - 1, 0), 0)),
            pl.BlockSpec((1, d), lambda i: (0, 0)),
            pl.BlockSpec(w.shape, lambda i: (0, 0)),
            pl.BlockSpec((DN_CONV, w3), lambda i: (0, 0)),
            pl.BlockSpec((1, LANES), lambda i: (0, 0)),
            pl.BlockSpec((1, LANES), lambda i: (0, 0)),
            pl.BlockSpec((tm, tm), lambda i: (0, 0)),
            pl.BlockSpec((tm, tm), lambda i: (0, 0)),
        ],
        out_specs=[
            pl.BlockSpec((tm, w3), lambda i: (i, 0)),
            pl.BlockSpec((tm, wgx), lambda i: (i, 0)),
            pl.BlockSpec((tm, LANES), lambda i: (i, 0)),
            pl.BlockSpec((tm, LANES), lambda i: (i, 0)),
        ],
        out_shape=[
            jax.ShapeDtypeStruct((t, w3), BF16),
            jax.ShapeDtypeStruct((t, wgx), BF16),
            jax.ShapeDtypeStruct((t, LANES), F32),
            jax.ShapeDtypeStruct((t, LANES), F32),
        ],
        compiler_params=_cparams("parallel"),
        name="dn_in",
    )(h, h, gain.reshape(1, d), w, conv_w, alog_row, dtb_row, tril, ones)


def _delta_kernel(q_ref, k_ref, v_ref, gate_ref, gcb_ref, gl_ref, gct_ref, og_ref, o_ref, s_ref, *, heads):
    h0 = pl.program_id(1) * heads
    rows = q_ref.shape[0]
    n_chunks = rows // DN_CHUNK
    hs = range(heads)

    @pl.when(pl.program_id(2) == 0)
    def _():
        s_ref[...] = jnp.zeros_like(s_ref)

    lane = lax.broadcasted_iota(I32, (rows, LANES), 1)
    gcb = gcb_ref[...]
    glb = gl_ref[...]
    gc_col = [jnp.sum(jnp.where(lane == h0 + h, gcb, 0.0), axis=1, keepdims=True) for h in hs]
    beta_col = [jnp.sum(jnp.where(lane == h0 + h + DN_HEADS, gcb, 0.0), axis=1, keepdims=True) for h in hs]
    gl_col = [jnp.sum(jnp.where(lane == h0 + h, glb, 0.0), axis=1, keepdims=True) for h in hs]

    ri = lax.broadcasted_iota(I32, (rows, rows), 0)
    ci = lax.broadcasted_iota(I32, (rows, rows), 1)
    same = (ri // DN_CHUNK) == (ci // DN_CHUNK)
    tri = same & (ci <= ri)
    strict = same & (ci < ri)
    eye = (ri == ci).astype(F32)
    cols = [slice(h * DN_HEAD, (h + 1) * DN_HEAD) for h in hs]

    decay = [jnp.exp(jnp.where(tri, gc_col[h] - gct_ref[h], -1e30)) for h in hs]
    k = [k_ref[:, cols[h]] for h in hs]
    q = [q_ref[:, cols[h]] for h in hs]
    kf = [k[h].astype(F32) for h in hs]
    eg = [jnp.exp(gc_col[h]) for h in hs]
    kb = [kf[h] * beta_col[h] for h in hs]
    rhs = [jnp.concatenate([v_ref[:, cols[h]].astype(F32) * beta_col[h], kb[h] * eg[h]], axis=1).astype(BF16) for h in hs]
    npow = [jnp.where(strict, -(_dot_nt(kb[h].astype(BF16), k[h]) * decay[h]), 0.0) for h in hs]
    inv = [eye + npow[h] for h in hs]
    for _ in range(5):
        nb = [npow[h].astype(BF16) for h in hs]
        npow = [_dot(nb[h], nb[h]) for h in hs]
        inv = [inv[h] + _dot(npow[h].astype(BF16), inv[h].astype(BF16)) for h in hs]
    uw = [_dot(inv[h].astype(BF16), rhs[h]) for h in hs]
    qk = [(_dot_nt(q[h], k[h]) * decay[h]).astype(BF16) for h in hs]
    w_b = [uw[h][:, DN_HEAD:].astype(BF16) for h in hs]
    q_dec = [(q[h].astype(F32) * eg[h]).astype(BF16) for h in hs]
    k_dec = [(kf[h] * jnp.exp(gl_col[h] - gc_col[h])).astype(BF16) for h in hs]

    state = [s_ref[h] for h in hs]
    outs = [[] for _ in hs]
    for c in range(n_chunks):
        rs = slice(c * DN_CHUNK, (c + 1) * DN_CHUNK)
        sb = [state[h].astype(BF16) for h in hs]
        ws = [_dot(jnp.concatenate([w_b[h][rs], q_dec[h][rs]], axis=0), sb[h]) for h in hs]
        v_new = [(uw[h][rs, :DN_HEAD] - ws[h][:DN_CHUNK]).astype(BF16) for h in hs]
        for h in hs:
            outs[h].append(ws[h][DN_CHUNK:] + _dot(qk[h][rs, rs], v_new[h]))
        state = [state[h] * jnp.exp(gl_col[h][c * DN_CHUNK:c * DN_CHUNK + 1, :]) + _dot_tn(k_dec[h][rs], v_new[h]) for h in hs]
    for h in hs:
        s_ref[h] = state[h]
        o = jnp.concatenate(outs[h], axis=0)
        gate = gate_ref[:, cols[h]].astype(F32)
        o_ref[:, cols[h]] = (_rms_rows(o, og_ref[...]) * (gate * _sigmoid(gate))).astype(o_ref.dtype)


def _delta_rule(qkvn, gx, gcb, gl, gct, o_gain_row, batch, seq):
    rows = min(DN_ROWS, seq)
    steps = seq // rows
    t = batch * seq
    heads = DN_HEADS_PER_STEP
    groups = DN_HEADS // heads
    w = heads * DN_HEAD
    return pl.pallas_call(
        functools.partial(_delta_kernel, heads=heads),
        grid=(batch, groups, steps),
        in_specs=[
            pl.BlockSpec((rows, w), lambda b, g, s: (b * steps + s, g)),
            pl.BlockSpec((rows, w), lambda b, g, s: (b * steps + s, groups + g)),
            pl.BlockSpec((rows, w), lambda b, g, s: (b * steps + s, 2 * groups + g)),
            pl.BlockSpec((rows, w), lambda b, g, s: (b * steps + s, g)),
            pl.BlockSpec((rows, LANES), lambda b, g, s: (b * steps + s, 0)),
            pl.BlockSpec((rows, LANES), lambda b, g, s: (b * steps + s, 0)),
            pl.BlockSpec((heads, 1, rows), lambda b, g, s: (g, 0, b * steps + s)),
            pl.BlockSpec((1, DN_HEAD), lambda b, g, s: (0, 0)),
        ],
        out_specs=pl.BlockSpec((rows, w), lambda b, g, s: (b * steps + s, g)),
        out_shape=jax.ShapeDtypeStruct((t, DN_WIDTH), BF16),
        scratch_shapes=[pltpu.VMEM((heads, DN_HEAD, DN_HEAD), F32)],
        compiler_params=_cparams("parallel", "parallel", "arbitrary"),
        name="delta_rule",
    )(qkvn, qkvn, qkvn, gx, gcb, gl, gct, o_gain_row)


def _router_kernel(h_ref, g_ref, rt_ref, us_ref, hn_ref, pos_ref, wts_ref, cstart_ref, carry_ref):
    i = pl.program_id(0)

    @pl.when(i == 0)
    def _():
        carry_ref[...] = jnp.zeros_like(carry_ref)

    hn = _rms_rows(h_ref[...], g_ref[...])
    hn_ref[...] = hn.astype(BF16)
    logits = lax.dot_general(rt_ref[...], hn, (((1,), (1,)), ((), ())),
                             precision=lax.Precision.HIGHEST, preferred_element_type=F32)
    sub = lax.broadcasted_iota(I32, logits.shape, 0)
    m1 = jnp.max(logits, axis=0, keepdims=True)
    i1 = jnp.min(jnp.where(logits == m1, sub, N_EXPERTS), axis=0, keepdims=True)
    rest = jnp.where(sub == i1, -jnp.inf, logits)
    m2 = jnp.max(rest, axis=0, keepdims=True)
    i2 = jnp.min(jnp.where(rest == m2, sub, N_EXPERTS), axis=0, keepdims=True)
    e21 = jnp.exp(m2 - m1)
    w1 = 1.0 / (1.0 + e21)
    w2 = e21 / (1.0 + e21)
    hit1 = sub == i1
    hit2 = sub == i2
    chosen = hit1 | hit2
    onehot = jnp.where(chosen, 1.0, 0.0)
    carry = carry_ref[...]
    cstart_ref[0] = carry
    before = _dot(onehot.astype(BF16), us_ref[...]) + carry[:, 0:1]
    carry_ref[...] = carry + jnp.sum(onehot, axis=1, keepdims=True)
    pos_ref[...] = jnp.where(chosen, before, -1.0)
    wts_ref[...] = jnp.where(hit1, w1, jnp.where(hit2, w2, 0.0))


def _router(h, gain, router_t):
    t, d = h.shape
    tm = min(MOE_BLK, t)
    nwin = t // tm
    r = jnp.arange(tm)[:, None]
    c = jnp.arange(tm)[None, :]
    us = (r < c).astype(BF16)
    return pl.pallas_call(
        _router_kernel,
        grid=(nwin,),
        in_specs=[
            pl.BlockSpec((tm, d), lambda i: (i, 0)),
            pl.BlockSpec((1, d), lambda i: (0, 0)),
            pl.BlockSpec((N_EXPERTS, d), lambda i: (0, 0)),
            pl.BlockSpec((tm, tm), lambda i: (0, 0)),
        ],
        out_specs=[
            pl.BlockSpec((tm, d), lambda i: (i, 0)),
            pl.BlockSpec((N_EXPERTS, tm), lambda i: (0, i)),
            pl.BlockSpec((N_EXPERTS, tm), lambda i: (0, i)),
            pl.BlockSpec((1, N_EXPERTS, LANES), lambda i: (i, 0, 0)),
        ],
        out_shape=[
            jax.ShapeDtypeStruct((t, d), BF16),
            jax.ShapeDtypeStruct((N_EXPERTS, t), F32),
            jax.ShapeDtypeStruct((N_EXPERTS, t), F32),
            jax.ShapeDtypeStruct((nwin, N_EXPERTS, LANES), F32),
        ],
        scratch_shapes=[pltpu.VMEM((N_EXPERTS, LANES), F32)],
        compiler_params=_cparams("arbitrary"),
        name="moe_router",
    )(h, gain.reshape(1, d), router_t, us)


def _dispatch_kernel(blk_ref, win_ref, flag_ref, exp_ref, *refs):
    edge_refs = refs[:MOE_SUBWINS + 1]
    hn_ref, slots_ref, wts_ref, xs_ref, ws_ref = refs[MOE_SUBWINS + 1:]
    l = pl.program_id(0)
    flag = flag_ref[l]
    nslot = xs_ref.shape[0]
    sub = hn_ref.shape[0] // MOE_SUBWINS

    @pl.when((flag & 2) != 0)
    def _():
        xs_ref[...] = jnp.zeros_like(xs_ref)
        ws_ref[...] = jnp.zeros_like(ws_ref)

    edges = [r[l] for r in edge_refs]
    for s in range(MOE_SUBWINS):
        for part in range(nslot // MOE_PART):
            r0 = part * MOE_PART

            @pl.when(((flag & 1) != 0) & (edges[s] < r0 + MOE_PART) & (edges[s + 1] > r0) & (edges[s + 1] > edges[s]))
            def _():
                e = exp_ref[l]
                base = (blk_ref[l] * nslot + r0).astype(F32)
                toks = slice(s * sub, (s + 1) * sub)
                rel = slots_ref[pl.ds(e, 1), toks] - base
                hit = rel == lax.broadcasted_iota(I32, (MOE_PART, sub), 0).astype(F32)
                rows = pl.ds(r0, MOE_PART)
                xs_ref[rows, :] += _dot(jnp.where(hit, 1.0, 0.0).astype(BF16), hn_ref[toks, :]).astype(xs_ref.dtype)
                ws_ref[rows, :] += jnp.sum(jnp.where(hit, wts_ref[pl.ds(e, 1), toks], 0.0), axis=1, keepdims=True)


def _dispatch(hn, slots_rows, wts, pairs, cap):
    t, d = hn.shape
    b = MOE_BLK
    w = MOE_SUBWINS * b
    return pl.pallas_call(
        _dispatch_kernel,
        grid_spec=pltpu.PrefetchScalarGridSpec(
            num_scalar_prefetch=len(pairs),
            grid=(pairs[0].shape[0],),
            in_specs=[
                pl.BlockSpec((w, d), lambda l, blk, win, *_: (win[l], 0)),
                pl.BlockSpec((N_EXPERTS, w), lambda l, blk, win, *_: (0, win[l])),
                pl.BlockSpec((N_EXPERTS, w), lambda l, blk, win, *_: (0, win[l])),
            ],
            out_specs=[
                pl.BlockSpec((b, d), lambda l, blk, win, *_: (blk[l], 0)),
                pl.BlockSpec((b, LANES), lambda l, blk, win, *_: (blk[l], 0)),
            ],
        ),
        out_shape=[jax.ShapeDtypeStruct((cap, d), BF16), jax.ShapeDtypeStruct((cap, LANES), F32)],
        compiler_params=_cparams("arbitrary"),
        name="moe_dispatch",
    )(*pairs, hn, slots_rows, wts)


def _experts_kernel(be_ref, src_ref, used_ref, xs_ref, ws_ref, wg_ref, wu_ref, wd_ref, y_ref, acc_ref):
    b = pl.program_id(0)
    f = pl.program_id(1)
    used = used_ref[b] != 0

    @pl.when(f == 0)
    def _():
        acc_ref[...] = jnp.zeros_like(acc_ref)

    @pl.when(used)
    def _():
        xs = xs_ref[...]
        g = _dot(xs, wg_ref[...])
        up = _dot(xs, wu_ref[...])
        act = (g * _sigmoid(g) * up).astype(BF16)
        acc_ref[...] += _dot(act, wd_ref[...])

    @pl.when(f == pl.num_programs(1) - 1)
    def _():
        y_ref[...] = jnp.where(used, acc_ref[...] * ws_ref[:, 0:1], 0.0).astype(y_ref.dtype)


def _experts(xs, ws, wg, wu, wd, blk_expert, blk_src, blk_used):
    cap, d = xs.shape
    ff = wg.shape[2]
    b = MOE_BLK
    tf = TF_MOE
    return pl.pallas_call(
        _experts_kernel,
        grid_spec=pltpu.PrefetchScalarGridSpec(
            num_scalar_prefetch=3,
            grid=(cap // b, ff // tf),
            in_specs=[
                pl.BlockSpec((b, d), lambda i, f, be, src, used: (src[i], 0)),
                pl.BlockSpec((b, LANES), lambda i, f, be, src, used: (src[i], 0)),
                pl.BlockSpec((None, d, tf), lambda i, f, be, src, used: (be[i], 0, f)),
                pl.BlockSpec((None, d, tf), lambda i, f, be, src, used: (be[i], 0, f)),
                pl.BlockSpec((None, tf, d), lambda i, f, be, src, used: (be[i], f, 0)),
            ],
            out_specs=pl.BlockSpec((b, d), lambda i, f, be, src, used: (i, 0)),
            scratch_shapes=[pltpu.VMEM((b, d), F32)],
        ),
        out_shape=jax.ShapeDtypeStruct((cap, d), BF16),
        compiler_params=_cparams("parallel", "arbitrary"),
        name="moe_experts",
    )(blk_expert, blk_src, blk_used, xs, ws, wg, wu, wd)


def _combine_kernel(blk_ref, win_ref, flag_ref, exp_ref, *refs):
    edge_refs = refs[:MOE_SUBWINS + 1]
    h_ref, y_ref, slots_ref, o_ref = refs[MOE_SUBWINS + 1:]
    l = pl.program_id(0)
    flag = flag_ref[l]

    @pl.when((flag & 2) != 0)
    def _():
        o_ref[...] = h_ref[...]

    nslot = y_ref.shape[0]
    sub = h_ref.shape[0] // MOE_SUBWINS
    edges = [r[l] for r in edge_refs]
    for s in range(MOE_SUBWINS):
        for part in range(nslot // MOE_PART):
            c0 = part * MOE_PART

            @pl.when(((flag & 1) != 0) & (edges[s] < c0 + MOE_PART) & (edges[s + 1] > c0) & (edges[s + 1] > edges[s]))
            def _():
                toks = pl.ds(s * sub, sub)
                cols = slots_ref[toks, :]
                mine = lax.broadcasted_iota(I32, cols.shape, 1) == exp_ref[l]
                base = (blk_ref[l] * nslot + c0).astype(F32)
                rel = jnp.sum(jnp.where(mine, cols, 0.0), axis=1, keepdims=True) - base
                hit = rel == lax.broadcasted_iota(I32, (sub, MOE_PART), 1).astype(F32)
                o_ref[toks, :] += _dot(jnp.where(hit, 1.0, 0.0).astype(BF16), y_ref[pl.ds(c0, MOE_PART), :])


def _combine(h, y, slots_cols, pairs):
    t, d = h.shape
    b = MOE_BLK
    w = MOE_SUBWINS * b
    return pl.pallas_call(
        _combine_kernel,
        grid_spec=pltpu.PrefetchScalarGridSpec(
            num_scalar_prefetch=len(pairs),
            grid=(pairs[0].shape[0],),
            in_specs=[
                pl.BlockSpec((w, d), lambda l, blk, win, *_: (win[l], 0)),
                pl.BlockSpec((b, d), lambda l, blk, win, *_: (blk[l], 0)),
                pl.BlockSpec((w, N_EXPERTS), lambda l, blk, win, *_: (win[l], 0)),
            ],
            out_specs=pl.BlockSpec((w, d), lambda l, blk, win, *_: (win[l], 0)),
        ),
        out_shape=jax.ShapeDtypeStruct((t, d), F32),
        compiler_params=_cparams("arbitrary"),
        name="moe_combine",
    )(*pairs, h, y, slots_cols)


def _pair_list(edges, expert_major, n_pairs):
    b = MOE_BLK
    n_e, n_w = edges[0].shape
    first_blk = edges[0] // b
    n_blk = jnp.where(edges[-1] > edges[0], (edges[-1] - 1) // b - first_blk + 1, 0)
    order = (lambda a: a.reshape(-1)) if expert_major else (lambda a: a.T.reshape(-1))
    win_of = order(jnp.broadcast_to(jnp.arange(n_w, dtype=I32)[None, :], (n_e, n_w)))
    exp_of = order(jnp.broadcast_to(jnp.arange(n_e, dtype=I32)[:, None], (n_e, n_w)))
    fb, nb = order(first_blk), order(n_blk)
    end = jnp.cumsum(nb)
    total = end[-1]
    l = jnp.arange(n_pairs, dtype=I32)
    lc = jnp.minimum(l, total - 1)
    p = jnp.sum((end[None, :] <= lc[:, None]).astype(I32), axis=1)
    onehot = jnp.arange(fb.shape[0], dtype=I32)[None, :] == p[:, None]

    def pick(v):
        return jnp.sum(jnp.where(onehot, v[None, :], 0), axis=1)

    blk = pick(fb) + (lc - (pick(end) - pick(nb)))
    win = pick(win_of)
    major = blk if expert_major else win
    first = jnp.concatenate([jnp.ones((1,), bool), major[1:] != major[:-1]])
    valid = l < total
    flag = valid.astype(I32) + 2 * (first & valid).astype(I32)
    rel = [jnp.clip(pick(order(e)) - blk * b, 0, b) for e in edges]
    return (blk, win, flag, pick(exp_of), *rel)


def _moe(h, gain, router, wg, wu, wd):
    t, d = h.shape
    b = MOE_BLK
    nwin = t // b
    nw = nwin // MOE_SUBWINS
    nblk = (2 * t) // b + N_EXPERTS
    cap = nblk * b
    n_pairs = nblk + N_EXPERTS * nw

    hn, pos, wts, cstart = _router(h, gain, router.T.astype(F32))

    cstart = cstart[:, :, 0].astype(I32).T
    routed = pos >= 0.0
    counts = jnp.sum(routed.astype(I32), axis=1)
    padded = (counts + b - 1) // b * b
    gend = jnp.cumsum(padded)
    gstart = gend - padded
    slots_rows = jnp.where(routed, pos + gstart[:, None].astype(F32), -1.0)
    slots_cols = slots_rows.T

    bounds = gstart[:, None] + jnp.concatenate([cstart, counts[:, None]], axis=1)
    edges = [bounds[:, k:k + MOE_SUBWINS * nw:MOE_SUBWINS] for k in range(MOE_SUBWINS + 1)]
    pairs_d = _pair_list(edges, True, n_pairs)
    pairs_c = _pair_list(edges, False, n_pairs)

    n_used = gend[-1] // b
    bidx = jnp.arange(nblk, dtype=I32)
    blk_used = (bidx < n_used).astype(I32)
    blk_src = jnp.minimum(bidx, n_used - 1)
    blk_expert = jnp.minimum(jnp.sum((gend[None, :] <= (blk_src * b)[:, None]).astype(I32), axis=1), N_EXPERTS - 1)

    xs, ws = _dispatch(hn, slots_rows, wts, pairs_d, cap)
    y = _experts(xs, ws, wg, wu, wd, blk_expert, blk_src, blk_used)
    return _combine(h, y, slots_cols, pairs_c)


def _tile_gain(gain, reps, scale=1.0):
    return (jnp.tile(gain.astype(F32), reps) * scale).reshape(1, -1)


def _mem_kv(mem2d, gain, w, k_gain):
    return _norm_proj(mem2d, gain, w.astype(BF16), _tile_gain(k_gain, XA_WIDTH // HEAD64), XA_WIDTH)[0]


def kernel(x, mem, mix_norm, ffn_norm, mem_norm, w_mem_kv, xa_q_norm, xa_k_norm, w_out, sb_w_in, sb_q_norm, sb_k_norm, dn_w_in, dn_conv, dn_a_log, dn_dt_bias, dn_o_norm, ffd_w_gate, ffd_w_up, ffd_w_down, moe_router, moe_w_gate, moe_w_up, moe_w_down):
    batch, seq, d = x.shape
    mem_len = mem.shape[1]
    t = batch * seq
    h = x.reshape(t, d)
    mem2d = mem.reshape(batch * mem_len, d)
    scale64 = HEAD64 ** -0.5

    qk_gain = jnp.concatenate([_tile_gain(sb_q_norm[0], SB_HEADS, scale64 * LOG2E), _tile_gain(sb_k_norm[0], SB_HEADS)], axis=1)
    proj0, qt0 = _norm_proj(h, mix_norm[0], sb_w_in[0].astype(BF16), qk_gain, 2 * SB_WIDTH, SB_WIDTH)
    mix0 = _sb_attention(proj0, qt0, batch, seq)
    memkv0 = _mem_kv(mem2d, mem_norm[0], w_mem_kv[0], xa_k_norm[0])
    xq_gain0 = _tile_gain(xa_q_norm[0], XA_WIDTH // HEAD64, scale64)
    h = _xattn_out(h, mix0, proj0, (3 * SB_WIDTH) // XA_WIDTH, memkv0, xq_gain0, w_out[0].astype(BF16), seq, mem_len)
    h = _ffn(h, ffn_norm[0], ffd_w_gate[0].astype(BF16), ffd_w_up[0].astype(BF16), ffd_w_down[0].astype(BF16))

    w1 = dn_w_in[0]
    w1 = jnp.concatenate([w1[:, :4 * DN_WIDTH], w1[:, 4 * DN_WIDTH + 2 * DN_HEADS:], w1[:, 4 * DN_WIDTH:4 * DN_WIDTH + 2 * DN_HEADS],
                          jnp.zeros((d, LANES - 2 * DN_HEADS), F32)], axis=1).astype(BF16)
    pad = LANES - DN_HEADS
    alog_row = jnp.pad(dn_a_log[0].astype(F32), (0, pad)).reshape(1, LANES)
    dtb_row = jnp.pad(dn_dt_bias[0].astype(F32), (0, pad)).reshape(1, LANES)
    qkvn, gx, gcb, gl = _dn_in(h, mix_norm[1], w1, dn_conv[0].astype(F32), alog_row, dtb_row, seq)
    gct = gcb[:, :DN_HEADS].T.reshape(DN_HEADS, 1, t)
    mix1 = _delta_rule(qkvn, gx, gcb, gl, gct, dn_o_norm[0].astype(F32).reshape(1, DN_HEAD), batch, seq)
    memkv1 = _mem_kv(mem2d, mem_norm[1], w_mem_kv[1], xa_k_norm[1])
    xq_gain1 = _tile_gain(xa_q_norm[1], XA_WIDTH // HEAD64, scale64)
    h = _xattn_out(h, mix1, gx, DN_WIDTH // XA_WIDTH, memkv1, xq_gain1, w_out[1].astype(BF16), seq, mem_len)
    h = _moe(h, ffn_norm[1], moe_router[0], moe_w_gate[0].astype(BF16), moe_w_up[0].astype(BF16), moe_w_down[0].astype(BF16))
    return h.reshape(batch, seq, d)
```
